```python
import jax, jax.numpy as jnp
from jax import lax
import numpy as np

D_MODEL = 2048
BATCH = 4
SEQ = 4096
DEPTH = 4

CHUNK = 64
N_A_LAYERS = DEPTH // 2
N_B_LAYERS = DEPTH - N_A_LAYERS
EPS = 1e-6
RET_HEADS = 8
RET_QK_DIM = D_MODEL // RET_HEADS
RET_V_DIM = 2 * RET_QK_DIM
RET_QK_WIDTH = RET_HEADS * RET_QK_DIM
RET_V_WIDTH = RET_HEADS * RET_V_DIM
RET_IN_WIDTH = 2 * RET_QK_WIDTH + 2 * RET_V_WIDTH
ROPE_BASE = 10000.0
ATT_HEADS = 16
ATT_HEAD_DIM = D_MODEL // ATT_HEADS
LEFT_CHUNKS = 8
BAND = (LEFT_CHUNKS + 1) * CHUNK
REL_CLIP = 128
N_REL = 2 * REL_CLIP + 1
PEER_HEADS = 8
PEER_NKEYS = 128
PEER_EXPERTS = PEER_NKEYS * PEER_NKEYS
PEER_QDIM = 256
PEER_HALF = PEER_QDIM // 2
PEER_TOPK = 16
PEER_TOKEN_BLOCK = 128

kernel_name = "yoco_retention_chunkattn_peer"


def rmsnorm(x, g):
    xf = x.astype(jnp.float32)
    y = xf * lax.rsqrt(jnp.mean(xf * xf, axis=-1, keepdims=True) + EPS)
    return (y * g.astype(jnp.float32)).astype(x.dtype)


def rotary(t, pos):
    half = t.shape[-1] // 2
    inv = 1.0 / (ROPE_BASE ** (jnp.arange(half, dtype=jnp.float32) / half))
    ang = pos[:, None] * inv[None, :]
    cos = jnp.cos(ang)[None, :, None, :]
    sin = jnp.sin(ang)[None, :, None, :]
    tf = t.astype(jnp.float32)
    t1, t2 = tf[..., :half], tf[..., half:]
    return jnp.concatenate([t1 * cos - t2 * sin, t1 * sin + t2 * cos], axis=-1).astype(t.dtype)


def retention(h, w_in, w_out, gn_g):
    B, S, _ = h.shape
    NC = S // CHUNK
    dt = h.dtype
    proj = h @ w_in
    q, k, v, g = jnp.split(proj, [RET_QK_WIDTH, 2 * RET_QK_WIDTH, 2 * RET_QK_WIDTH + RET_V_WIDTH], axis=-1)
    pos = jnp.arange(S, dtype=jnp.float32)
    q = rotary(q.reshape(B, S, RET_HEADS, RET_QK_DIM), pos)
    k = rotary(k.reshape(B, S, RET_HEADS, RET_QK_DIM), pos) * (RET_QK_DIM ** -0.5)
    v = v.reshape(B, S, RET_HEADS, RET_V_DIM)
    qc = q.reshape(B, NC, CHUNK, RET_HEADS, RET_QK_DIM).transpose(0, 1, 3, 2, 4)
    kc = k.reshape(B, NC, CHUNK, RET_HEADS, RET_QK_DIM).transpose(0, 1, 3, 2, 4)
    vc = v.reshape(B, NC, CHUNK, RET_HEADS, RET_V_DIM).transpose(0, 1, 3, 2, 4)
    log_gamma = jnp.log1p(-jnp.exp2(-5.0 - jnp.arange(RET_HEADS, dtype=jnp.float32)))
    idx = jnp.arange(CHUNK, dtype=jnp.float32)
    diff = idx[:, None] - idx[None, :]
    decay_mask = jnp.where(diff >= 0, jnp.exp(log_gamma[:, None, None] * jnp.maximum(diff, 0.0)), 0.0).astype(dt)
    xi = jnp.exp(log_gamma[:, None] * (idx[None, :] + 1.0)).astype(dt)
    zeta = jnp.exp(log_gamma[:, None] * (CHUNK - 1.0 - idx[None, :])).astype(dt)
    chunk_decay = jnp.exp(log_gamma * CHUNK).astype(dt)
    scores = jnp.einsum('bnhid,bnhjd->bnhij', qc, kc) * decay_mask
    intra = jnp.einsum('bnhij,bnhje->bnhie', scores, vc)
    def step(R, inp):
        q_, kz, v_ = inp
        cross = jnp.einsum('bhcd,bhde->bhce', q_, R)
        R = R * chunk_decay[None, :, None, None] + jnp.einsum('bhcd,bhce->bhde', kz, v_)
        return R, cross
    xs = (qc.transpose(1, 0, 2, 3, 4),
          (kc * zeta[None, None, :, :, None]).transpose(1, 0, 2, 3, 4),
          vc.transpose(1, 0, 2, 3, 4))
    R0 = jnp.zeros((B, RET_HEADS, RET_QK_DIM, RET_V_DIM), dt)
    _, cross = lax.scan(step, R0, xs)
    cross = cross.transpose(1, 0, 2, 3, 4) * xi[None, None, :, :, None]
    o = (intra + cross).transpose(0, 1, 3, 2, 4).reshape(B, S, RET_HEADS, RET_V_DIM)
    of = o.astype(jnp.float32)
    mu = jnp.mean(of, axis=-1, keepdims=True)
    var = jnp.mean(jnp.square(of - mu), axis=-1, keepdims=True)
    y = ((of - mu) * lax.rsqrt(var + EPS)).reshape(B, S, RET_V_WIDTH) * gn_g.astype(jnp.float32)
    y = y.astype(dt)
    return (jax.nn.silu(g) * y) @ w_out


def shared_kv(x, g_kv, w_kv):
    B, S, _ = x.shape
    kv = rmsnorm(x, g_kv) @ w_kv
    k, v = jnp.split(kv, 2, axis=-1)
    pad = ((0, 0), (LEFT_CHUNKS * CHUNK, 0), (0, 0), (0, 0))
    k = jnp.pad(k.reshape(B, S, ATT_HEADS, ATT_HEAD_DIM), pad)
    v = jnp.pad(v.reshape(B, S, ATT_HEADS, ATT_HEAD_DIM), pad)
    return k, v


def chunk_attention(h, k_pad, v_pad, w_q, w_o, rel_bias):
    B, S, D = h.shape
    NC = S // CHUNK
    q = (h @ w_q).reshape(B, NC, CHUNK, ATT_HEADS, ATT_HEAD_DIM) * (ATT_HEAD_DIM ** -0.5)
    q = q.transpose(1, 0, 2, 3, 4)
    i_idx = jnp.arange(CHUNK)[:, None]
    m_idx = jnp.arange(BAND)
    rel = jnp.clip(LEFT_CHUNKS * CHUNK + i_idx - m_idx[None, :], -REL_CLIP, REL_CLIP) + REL_CLIP
    bias = rel_bias.astype(jnp.float32)[:, rel]

    def one_chunk(args):
        c, qc = args
        kb = lax.dynamic_slice_in_dim(k_pad, c * CHUNK, BAND, axis=1)
        vb = lax.dynamic_slice_in_dim(v_pad, c * CHUNK, BAND, axis=1)
        s = jnp.einsum('bihd,bjhd->bhij', qc, kb).astype(jnp.float32) + bias
        valid = m_idx >= (LEFT_CHUNKS - c) * CHUNK
        s = jnp.where(valid[None, None, None, :], s, -1e30)
        p = jax.nn.softmax(s, axis=-1).astype(vb.dtype)
        return jnp.einsum('bhij,bjhd->bihd', p, vb)

    out = lax.map(one_chunk, (jnp.arange(NC), q))
    out = out.transpose(1, 0, 2, 3, 4).reshape(B, S, D)
    return out @ w_o


def peer(h, w_q, sub_keys, u, v):
    B, S, D = h.shape
    T = B * S
    hf = h.reshape(T, D)
    q = (hf @ w_q).reshape(T, PEER_HEADS, 2, PEER_HALF)
    s = jnp.einsum('thpd,pkd->thpk', q, sub_keys).astype(jnp.float32)
    s1, i1 = lax.top_k(s[:, :, 0], PEER_TOPK)
    s2, i2 = lax.top_k(s[:, :, 1], PEER_TOPK)
    cand_s = (s1[..., :, None] + s2[..., None, :]).reshape(T, PEER_HEADS, PEER_TOPK * PEER_TOPK)
    cand_i = (i1[..., :, None] * PEER_NKEYS + i2[..., None, :]).reshape(T, PEER_HEADS, PEER_TOPK * PEER_TOPK)
    top_s, pos = lax.top_k(cand_s, PEER_TOPK)
    eidx = jnp.take_along_axis(cand_i, pos, axis=-1)
    w = jax.nn.softmax(top_s, axis=-1).astype(h.dtype)
    nblk = T // PEER_TOKEN_BLOCK
    hb = hf.reshape(nblk, PEER_TOKEN_BLOCK, D)
    eb = eidx.reshape(nblk, PEER_TOKEN_BLOCK, PEER_HEADS * PEER_TOPK)
    wb = w.reshape(nblk, PEER_TOKEN_BLOCK, PEER_HEADS * PEER_TOPK)

    def block(args):
        hb_, idx_, w_ = args
        ug = jnp.take(u, idx_, axis=0)
        a = jax.nn.gelu(jnp.einsum('td,ted->te', hb_, ug), approximate=False) * w_
        vg = jnp.take(v, idx_, axis=0)
        return jnp.einsum('te,ted->td', a, vg)

    out = lax.map(block, (hb, eb, wb))
    return out.reshape(B, S, D)


def setup_inputs(seed: int = 0) -> dict:
    key = jax.random.key(seed)
    ks = jax.random.split(key, 16)
    f32 = jnp.float32
    D = D_MODEL
    nrm = lambda k, shape, scale: jax.random.normal(k, shape, f32) * scale
    return {
        "x": nrm(ks[0], (BATCH, SEQ, D), 1.0),
        "ln_mix": 1.0 + nrm(ks[1], (DEPTH, D), 0.02),
        "ln_ffn": 1.0 + nrm(ks[2], (DEPTH, D), 0.02),
        "ret_w_in": nrm(ks[3], (N_A_LAYERS, D, RET_IN_WIDTH), D ** -0.5),
        "ret_w_out": nrm(ks[4], (N_A_LAYERS, RET_V_WIDTH, D), RET_V_WIDTH ** -0.5),
        "ret_gn": 1.0 + nrm(ks[5], (N_A_LAYERS, RET_V_WIDTH), 0.02),
        "kv_norm": 1.0 + nrm(ks[6], (D,), 0.02),
        "w_kv": nrm(ks[7], (D, 2 * D), D ** -0.5),
        "att_w_q": nrm(ks[8], (N_B_LAYERS, D, D), D ** -0.5),
        "att_w_o": nrm(ks[9], (N_B_LAYERS, D, D), D ** -0.5),
        "att_rel_bias": nrm(ks[10], (N_B_LAYERS, ATT_HEADS, N_REL), 0.1),
        "peer_w_q": nrm(ks[11], (DEPTH, D, PEER_HEADS * PEER_QDIM), D ** -0.5),
        "peer_sub_keys": nrm(ks[12], (DEPTH, 2, PEER_NKEYS, PEER_HALF), PEER_HALF ** -0.5),
        "peer_u": nrm(ks[13], (DEPTH, PEER_EXPERTS, D), D ** -0.5),
        "peer_v": nrm(ks[14], (DEPTH, PEER_EXPERTS, D), (PEER_HEADS * PEER_TOPK) ** -0.5),
        "ln_final": 1.0 + nrm(ks[15], (D,), 0.02),
    }


def reference(x, ln_mix, ln_ffn, ret_w_in, ret_w_out, ret_gn, kv_norm, w_kv,
              att_w_q, att_w_o, att_rel_bias, peer_w_q, peer_sub_keys, peer_u, peer_v,
              ln_final):
    k_pad = None
    v_pad = None
    for l in range(DEPTH):
        h = rmsnorm(x, ln_mix[l])
        if l < N_A_LAYERS:
            x = x + retention(h, ret_w_in[l], ret_w_out[l], ret_gn[l])
        else:
            j = l - N_A_LAYERS
            x = x + chunk_attention(h, k_pad, v_pad, att_w_q[j], att_w_o[j], att_rel_bias[j])
        x = x + peer(rmsnorm(x, ln_ffn[l]), peer_w_q[l], peer_sub_keys[l], peer_u[l], peer_v[l])
        if l == N_A_LAYERS - 1:
            k_pad, v_pad = shared_kv(x, kv_norm, w_kv)
    return rmsnorm(x, ln_final)
```

```python
import functools

import jax
import jax.numpy as jnp
from jax import lax
from jax.experimental import pallas as pl
from jax.experimental.pallas import tpu as pltpu

F32 = jnp.float32
BF16 = jnp.bfloat16

EPS = 1e-6
ROPE_BASE = 10000.0
REF_CHUNK = 64
RET_HEADS = 8
RET_QK_DIM = 256
RET_V_DIM = 512
RET_BLOCK = 256
ATT_HEADS = 16
ATT_HEAD_DIM = 128
LEFT_CHUNKS = 8
REL_CLIP = 128
ATT_QBLOCK = 256
ATT_WINDOW = ATT_QBLOCK + LEFT_CHUNKS * REF_CHUNK
PEER_HEADS = 8
PEER_NKEYS = 128
PEER_HALF = 128
PEER_TOPK = 16
MASK_VALUE = -1e30
NEG_INF = float("-inf")

V7X_LANES = 128
V7X_VMEM_LIMIT_BYTES = 56 * 1024 * 1024

NT_DIMS = (((1,), (1,)), ((), ()))
TN_DIMS = (((0,), (0,)), ((), ()))


def _params(*sem):
    return pltpu.CompilerParams(dimension_semantics=sem, vmem_limit_bytes=V7X_VMEM_LIMIT_BYTES)


def _rmsnorm_rows(x, g):
    ms = jnp.mean(x * x, axis=-1, keepdims=True)
    return x * lax.rsqrt(ms + EPS) * g


def _norm_matmul_kernel(x_ref, g_ref, w_ref, o_ref, hn_ref, *, scale):
    @pl.when(pl.program_id(1) == 0)
    def _():
        hn_ref[...] = _rmsnorm_rows(x_ref[...], g_ref[...]).astype(BF16)

    acc = jnp.dot(hn_ref[...], w_ref[...], preferred_element_type=F32)
    if scale != 1.0:
        acc = acc * scale
    o_ref[...] = acc.astype(o_ref.dtype)


def norm_matmul(x, gain, w, *, col_start, n_cols, out_dtype, scale=1.0, tm=1024, tn=1024):
    t, d = x.shape
    tm = min(tm, t)
    tn = min(tn, n_cols)
    assert t % tm == 0 and n_cols % tn == 0 and col_start % tn == 0
    off = col_start // tn
    return pl.pallas_call(
        functools.partial(_norm_matmul_kernel, scale=scale),
        grid=(t // tm, n_cols // tn),
        in_specs=[
            pl.BlockSpec((tm, d), lambda i, j: (i, 0)),
            pl.BlockSpec((1, d), lambda i, j: (0, 0)),
            pl.BlockSpec((d, tn), lambda i, j: (0, j + off)),
        ],
        out_specs=pl.BlockSpec((tm, tn), lambda i, j: (i, j)),
        out_shape=jax.ShapeDtypeStruct((t, n_cols), out_dtype),
        scratch_shapes=[pltpu.VMEM((tm, d), BF16)],
        compiler_params=_params("parallel", "arbitrary"),
        name="norm_matmul",
    )(x, gain.reshape(1, d), w)


def _matmul_residual_kernel(y_ref, w_ref, x_ref, o_ref):
    o_ref[...] = x_ref[...] + jnp.dot(y_ref[...], w_ref[...], preferred_element_type=F32)


def matmul_residual(y, w, x, *, tm=1024, tn=512):
    t, k = y.shape
    n = w.shape[1]
    tm = min(tm, t)
    tn = min(tn, n)
    assert t % tm == 0 and n % tn == 0
    return pl.pallas_call(
        _matmul_residual_kernel,
        grid=(t // tm, n // tn),
        in_specs=[
            pl.BlockSpec((tm, k), lambda i, j: (i, 0)),
            pl.BlockSpec((k, tn), lambda i, j: (0, j)),
            pl.BlockSpec((tm, tn), lambda i, j: (i, j)),
        ],
        out_specs=pl.BlockSpec((tm, tn), lambda i, j: (i, j)),
        out_shape=jax.ShapeDtypeStruct((t, n), F32),
        compiler_params=_params("parallel", "arbitrary"),
        name="matmul_residual",
    )(y, w, x)


def _retention_tables(seq):
    c = RET_BLOCK
    log_gamma = jnp.log1p(-jnp.exp2(-5.0 - jnp.arange(RET_HEADS, dtype=F32)))
    idx = jnp.arange(c, dtype=F32)
    diff = idx[:, None] - idx[None, :]
    dmask = jnp.where(diff >= 0, jnp.exp(log_gamma[:, None, None] * jnp.maximum(diff, 0.0)), 0.0)
    xi = jnp.exp(log_gamma[:, None] * (idx[None, :] + 1.0))
    zeta = jnp.exp(log_gamma[:, None] * (c - 1.0 - idx[None, :]))
    cdecay = jnp.exp(log_gamma * c)
    xi = jnp.broadcast_to(xi[:, :, None], (RET_HEADS, c, V7X_LANES))
    zeta = jnp.broadcast_to(zeta[:, :, None], (RET_HEADS, c, V7X_LANES))
    cdecay = jnp.broadcast_to(cdecay[:, None, None], (RET_HEADS, 1, RET_V_DIM))
    half = RET_QK_DIM // 2
    inv = 1.0 / (ROPE_BASE ** (jnp.arange(half, dtype=F32) / half))
    ang = jnp.arange(seq, dtype=F32)[:, None] * inv[None, :]
    return dmask, xi, zeta, cdecay, jnp.cos(ang), jnp.sin(ang)


def _retention_kernel(q_ref, k_ref, v_ref, g_ref, cos_ref, sin_ref, dmask_ref, xi_ref, zeta_ref, cd_ref,
                      gn_ref, y_ref, state_ref):
    @pl.when(pl.program_id(2) == 0)
    def _():
        state_ref[...] = jnp.zeros_like(state_ref)

    cos = cos_ref[...]
    sin = sin_ref[...]
    half = RET_QK_DIM // 2

    def rotate(t):
        t1 = t[:, :half]
        t2 = t[:, half:]
        return jnp.concatenate([t1 * cos - t2 * sin, t1 * sin + t2 * cos], axis=-1)

    q = rotate(q_ref[...])
    k = rotate(k_ref[...]) * (RET_QK_DIM ** -0.5)
    qb = q.astype(BF16)
    kb = k.astype(BF16)
    v = v_ref[...]

    scores = lax.dot_general(qb, kb, NT_DIMS, preferred_element_type=F32) * dmask_ref[...]
    intra = jnp.dot(scores.astype(BF16), v, preferred_element_type=F32)
    state = state_ref[...]
    xi = jnp.concatenate([xi_ref[...]] * (RET_V_DIM // V7X_LANES), axis=-1)
    cross = jnp.dot(qb, state.astype(BF16), preferred_element_type=F32) * xi
    o = intra + cross

    zeta = jnp.concatenate([zeta_ref[...]] * (RET_QK_DIM // V7X_LANES), axis=-1)
    kz = (k * zeta).astype(BF16)
    state_ref[...] = state * cd_ref[...] + lax.dot_general(kz, v, TN_DIMS, preferred_element_type=F32)

    mu = jnp.mean(o, axis=-1, keepdims=True)
    d = o - mu
    var = jnp.mean(d * d, axis=-1, keepdims=True)
    y = d * lax.rsqrt(var + EPS) * gn_ref[...]
    g = g_ref[...]
    y_ref[...] = (g * jax.nn.sigmoid(g) * y).astype(y_ref.dtype)


def retention_core(qk, v, g, gn, batch, seq):
    c = RET_BLOCK
    nc = seq // c
    dmask, xi, zeta, cdecay, cos, sin = _retention_tables(seq)
    row = lambda b, h, n: (b * nc + n, h)
    per_head = lambda b, h, n: (h, 0, 0)
    return pl.pallas_call(
        _retention_kernel,
        grid=(batch, RET_HEADS, nc),
        in_specs=[
            pl.BlockSpec((c, RET_QK_DIM), row),
            pl.BlockSpec((c, RET_QK_DIM), lambda b, h, n: (b * nc + n, RET_HEADS + h)),
            pl.BlockSpec((c, RET_V_DIM), row),
            pl.BlockSpec((c, RET_V_DIM), row),
            pl.BlockSpec((c, RET_QK_DIM // 2), lambda b, h, n: (n, 0)),
            pl.BlockSpec((c, RET_QK_DIM // 2), lambda b, h, n: (n, 0)),
            pl.BlockSpec((None, c, c), per_head),
            pl.BlockSpec((None, c, V7X_LANES), per_head),
            pl.BlockSpec((None, c, V7X_LANES), per_head),
            pl.BlockSpec((None, 1, RET_V_DIM), per_head),
            pl.BlockSpec((1, RET_V_DIM), lambda b, h, n: (0, h)),
        ],
        out_specs=pl.BlockSpec((c, RET_V_DIM), row),
        out_shape=jax.ShapeDtypeStruct(v.shape, BF16),
        scratch_shapes=[pltpu.VMEM((RET_QK_DIM, RET_V_DIM), F32)],
        compiler_params=_params("parallel", "parallel", "arbitrary"),
        name="retention_core",
    )(qk, qk, v, g, cos, sin, dmask, xi, zeta, cdecay, gn.reshape(1, -1))


def _attention_bias(rel_bias):
    i = jnp.arange(ATT_QBLOCK)[:, None]
    m = jnp.arange(ATT_WINDOW)[None, :]
    rel = jnp.clip(i - m + LEFT_CHUNKS * REF_CHUNK, -REL_CLIP, REL_CLIP) + REL_CLIP
    ci = i // REF_CHUNK
    cm = m // REF_CHUNK
    band = (cm >= ci) & (cm <= ci + LEFT_CHUNKS)
    return jnp.where(band[None], rel_bias.astype(F32)[:, rel], MASK_VALUE)


def _attention_kernel(q_ref, k0_ref, k1_ref, k2_ref, v0_ref, v1_ref, v2_ref, bias_ref, o_ref):
    qblk = pl.program_id(1)
    n_prev = LEFT_CHUNKS * REF_CHUNK // ATT_QBLOCK
    col = lax.broadcasted_iota(jnp.int32, (1, ATT_WINDOW), 1)
    start_mask = jnp.where(col >= (n_prev - qblk) * ATT_QBLOCK, 0.0, MASK_VALUE)
    k_refs = (k0_ref, k1_ref, k2_ref)
    v_refs = (v0_ref, v1_ref, v2_ref)

    def one_head(h, carry):
        cols = pl.ds(pl.multiple_of(h * ATT_HEAD_DIM, ATT_HEAD_DIM), ATT_HEAD_DIM)
        q = q_ref[:, cols]
        s = jnp.concatenate(
            [lax.dot_general(q, kr[:, cols], NT_DIMS, preferred_element_type=F32) for kr in k_refs], axis=-1)
        s = s + bias_ref[h] + start_mask
        m = jnp.max(s, axis=-1, keepdims=True)
        p = jnp.exp(s - m)
        p = (p / jnp.sum(p, axis=-1, keepdims=True)).astype(BF16)
        o = jnp.dot(p[:, :ATT_QBLOCK], v_refs[0][:, cols], preferred_element_type=F32)
        for w in range(1, len(v_refs)):
            o = o + jnp.dot(p[:, w * ATT_QBLOCK:(w + 1) * ATT_QBLOCK], v_refs[w][:, cols],
                            preferred_element_type=F32)
        o_ref[:, cols] = o.astype(o_ref.dtype)
        return carry

    lax.fori_loop(0, ATT_HEADS, one_head, 0)


def attention_core(q, kv, bias, batch, seq):
    t, d = q.shape
    nq = seq // ATT_QBLOCK
    n_prev = LEFT_CHUNKS * REF_CHUNK // ATT_QBLOCK
    assert ATT_WINDOW == (n_prev + 1) * ATT_QBLOCK

    def window(w, part):
        return pl.BlockSpec((ATT_QBLOCK, d), lambda b, n: (b * nq + jnp.maximum(n - n_prev + w, 0), part))

    return pl.pallas_call(
        _attention_kernel,
        grid=(batch, nq),
        in_specs=[pl.BlockSpec((ATT_QBLOCK, d), lambda b, n: (b * nq + n, 0))]
        + [window(w, 0) for w in range(n_prev + 1)]
        + [window(w, 1) for w in range(n_prev + 1)]
        + [pl.BlockSpec(bias.shape, lambda b, n: (0, 0, 0))],
        out_specs=pl.BlockSpec((ATT_QBLOCK, d), lambda b, n: (b * nq + n, 0)),
        out_shape=jax.ShapeDtypeStruct((t, d), BF16),
        compiler_params=_params("parallel", "arbitrary"),
        name="attention_core",
    )(q, kv, kv, kv, kv, kv, kv, bias)


PEER_RANKS = PEER_TOPK + 1
PEER_RANK_ROWS = 24
PEER_CAND_ROWS = PEER_RANK_ROWS + 8 * (PEER_RANKS - 1)


def _extract_max(work_ref):
    w = work_ref[...]
    m = jnp.max(w, axis=0, keepdims=True)
    work_ref[...] = jnp.where(w == m, NEG_INF, w)
    return m


def _peer_prep_kernel(x_ref, g_ref, wqt_ref, sk_ref, hn_ref, s2_ref, e2_ref, thr_ref, c_ref,
                      qt_ref, s1_ref, work_ref, top_ref, cand_ref):
    hn = _rmsnorm_rows(x_ref[...], g_ref[...]).astype(BF16)
    hn_ref[...] = hn
    qt_ref[...] = lax.dot_general(wqt_ref[...], hn, NT_DIMS, preferred_element_type=F32)
    top_ref[...] = jnp.full(top_ref.shape, NEG_INF, F32)

    def one_head(h, carry):
        for part in range(2):
            rows = pl.ds(pl.multiple_of(h * (2 * PEER_HALF) + part * PEER_HALF, PEER_HALF), PEER_HALF)
            s = jnp.dot(sk_ref[part], qt_ref[rows, :].astype(BF16), preferred_element_type=F32)
            if part == 0:
                s1_ref[...] = s
            else:
                s2_ref[h] = s
            work_ref[...] = s

            def pop(r, c2, part=part):
                top_ref[part, pl.ds(r, 1), :] = _extract_max(work_ref)
                return c2

            lax.fori_loop(0, PEER_RANKS, pop, 0)

        xs = top_ref[0]
        ys = top_ref[1]
        cand_ref[0:PEER_RANK_ROWS, :] = xs[0:1, :] + ys
        for i in range(1, PEER_RANKS):
            cand_ref[PEER_RANK_ROWS + 8 * (i - 1):PEER_RANK_ROWS + 8 * i, :] = xs[i:i + 1, :] + ys[0:8, :]
        best = xs[0:1, :] + ys[0:1, :]

        top_sums = [_extract_max(cand_ref) for _ in range(PEER_RANKS)]
        z = jnp.exp(top_sums[0] - best)
        for v in top_sums[1:PEER_TOPK]:
            z = z + jnp.exp(v - best)
        tau = 0.5 * (top_sums[PEER_TOPK - 1] + top_sums[PEER_TOPK])
        s1 = s1_ref[...]
        thr_ref[h] = tau - s1
        c_ref[h] = jnp.exp(s1 - xs[0:1, :]) / z
        e2_ref[h] = jnp.exp(s2_ref[h] - ys[0:1, :])
        return carry

    lax.fori_loop(0, PEER_HEADS, one_head, 0)


def peer_prep(x, gain, wq_t, sub_keys, *, tm=512):
    t, d = x.shape
    tm = min(tm, t)
    assert t % tm == 0
    nq = wq_t.shape[0]
    fac = jax.ShapeDtypeStruct((PEER_HEADS, PEER_NKEYS, t), F32)
    fac_spec = pl.BlockSpec((PEER_HEADS, PEER_NKEYS, tm), lambda i: (0, 0, i))
    return pl.pallas_call(
        _peer_prep_kernel,
        grid=(t // tm,),
        in_specs=[
            pl.BlockSpec((tm, d), lambda i: (i, 0)),
            pl.BlockSpec((1, d), lambda i: (0, 0)),
            pl.BlockSpec((nq, d), lambda i: (0, 0)),
            pl.BlockSpec(sub_keys.shape, lambda i: (0, 0, 0)),
        ],
        out_specs=[pl.BlockSpec((tm, d), lambda i: (i, 0)), fac_spec, fac_spec, fac_spec, fac_spec],
        out_shape=[jax.ShapeDtypeStruct((t, d), BF16), fac, fac, fac, fac],
        scratch_shapes=[
            pltpu.VMEM((nq, tm), F32),
            pltpu.VMEM((PEER_NKEYS, tm), F32),
            pltpu.VMEM((PEER_NKEYS, tm), F32),
            pltpu.VMEM((2, PEER_RANK_ROWS, tm), F32),
            pltpu.VMEM((PEER_CAND_ROWS, tm), F32),
        ],
        compiler_params=_params("parallel"),
        name="peer_prep",
    )(x, gain.reshape(1, d), wq_t, sub_keys)


PEER_KEYS_PER_STEP = 8
PEER_ETILE = PEER_KEYS_PER_STEP * PEER_NKEYS
PEER_SUBROWS = 32


def _peer_main_kernel(hn_ref, u_ref, vt_ref, s2_ref, e2_ref, thr_ref, c_ref, x_ref, o_ref,
                      acc_ref, at_ref, p_ref):
    j = pl.program_id(1)
    tm = hn_ref.shape[0]

    @pl.when(j == 0)
    def _():
        acc_ref[...] = jnp.zeros_like(acc_ref)

    at_ref[...] = lax.dot_general(u_ref[...], hn_ref[...], NT_DIMS, preferred_element_type=F32)

    for ka in range(PEER_KEYS_PER_STEP):
        for tg in range(tm // V7X_LANES):
            lanes = slice(tg * V7X_LANES, (tg + 1) * V7X_LANES)
            thr = [thr_ref[h, ka:ka + 1, lanes] for h in range(PEER_HEADS)]
            cw = [c_ref[h, ka:ka + 1, lanes] for h in range(PEER_HEADS)]
            for bg in range(PEER_NKEYS // PEER_SUBROWS):
                brows = slice(bg * PEER_SUBROWS, (bg + 1) * PEER_SUBROWS)
                w = None
                for h in range(PEER_HEADS):
                    term = jnp.where(s2_ref[h, brows, lanes] >= thr[h], e2_ref[h, brows, lanes] * cw[h], 0.0)
                    w = term if w is None else w + term
                erows = slice(ka * PEER_NKEYS + bg * PEER_SUBROWS, ka * PEER_NKEYS + (bg + 1) * PEER_SUBROWS)
                pre = at_ref[erows, lanes]
                act = 0.5 * pre * (1.0 + lax.erf(pre * (2.0 ** -0.5)))
                p_ref[erows, lanes] = (act * w).astype(BF16)

    acc_ref[...] += jnp.dot(vt_ref[...], p_ref[...], preferred_element_type=F32)

    @pl.when(j == pl.num_programs(1) - 1)
    def _():
        o_ref[...] = x_ref[...] + acc_ref[...].T


def peer_main(hn, u, v_t, s2, e2, thr, c, x, *, tm=512):
    t, d = x.shape
    n_exp = u.shape[0]
    tm = min(tm, t)
    assert t % tm == 0 and n_exp % PEER_ETILE == 0 and n_exp == PEER_NKEYS * PEER_NKEYS
    all_keys = pl.BlockSpec((PEER_HEADS, PEER_NKEYS, tm), lambda i, j: (0, 0, i))
    step_keys = pl.BlockSpec((PEER_HEADS, PEER_KEYS_PER_STEP, tm), lambda i, j: (0, j, i))
    return pl.pallas_call(
        _peer_main_kernel,
        grid=(t // tm, n_exp // PEER_ETILE),
        in_specs=[
            pl.BlockSpec((tm, d), lambda i, j: (i, 0)),
            pl.BlockSpec((PEER_ETILE, d), lambda i, j: (j, 0)),
            pl.BlockSpec((d, PEER_ETILE), lambda i, j: (0, j)),
            all_keys, all_keys, step_keys, step_keys,
            pl.BlockSpec((tm, d), lambda i, j: (i, 0), pipeline_mode=pl.Buffered(1)),
        ],
        out_specs=pl.BlockSpec((tm, d), lambda i, j: (i, 0)),
        out_shape=jax.ShapeDtypeStruct((t, d), F32),
        scratch_shapes=[
            pltpu.VMEM((d, tm), F32),
            pltpu.VMEM((PEER_ETILE, tm), F32),
            pltpu.VMEM((PEER_ETILE, tm), BF16),
        ],
        compiler_params=_params("parallel", "arbitrary"),
        name="peer_main",
    )(hn, u, v_t, s2, e2, thr, c, x)


def peer_layer(x, gain, wq_t, sub_keys, u, v_t):
    hn, s2, e2, thr, c = peer_prep(x, gain, wq_t, sub_keys)
    return peer_main(hn, u, v_t, s2, e2, thr, c, x)


def _final_norm_kernel(x_ref, g_ref, o_ref):
    o_ref[...] = _rmsnorm_rows(x_ref[...], g_ref[...])


def final_norm(x, gain, *, tm=1024):
    t, d = x.shape
    tm = min(tm, t)
    return pl.pallas_call(
        _final_norm_kernel,
        grid=(t // tm,),
        in_specs=[pl.BlockSpec((tm, d), lambda i: (i, 0)), pl.BlockSpec((1, d), lambda i: (0, 0))],
        out_specs=pl.BlockSpec((tm, d), lambda i: (i, 0)),
        out_shape=jax.ShapeDtypeStruct((t, d), F32),
        compiler_params=_params("parallel"),
        name="final_norm",
    )(x, gain.reshape(1, d))


def kernel(x, ln_mix, ln_ffn, ret_w_in, ret_w_out, ret_gn, kv_norm, w_kv, att_w_q, att_w_o, att_rel_bias,
           peer_w_q, peer_sub_keys, peer_u, peer_v, ln_final):
    batch, seq, d = x.shape
    depth = ln_mix.shape[0]
    n_a = ret_w_in.shape[0]
    xt = x.reshape(batch * seq, d)
    qk_width = 2 * RET_HEADS * RET_QK_DIM
    v_width = RET_HEADS * RET_V_DIM
    kv = None
    for l in range(depth):
        if l < n_a:
            w_in = ret_w_in[l].astype(BF16)
            qk = norm_matmul(xt, ln_mix[l], w_in, col_start=0, n_cols=qk_width, out_dtype=F32)
            v = norm_matmul(xt, ln_mix[l], w_in, col_start=qk_width, n_cols=v_width, out_dtype=BF16)
            g = norm_matmul(xt, ln_mix[l], w_in, col_start=qk_width + v_width, n_cols=v_width, out_dtype=F32)
            y = retention_core(qk, v, g, ret_gn[l], batch, seq)
            xt = matmul_residual(y, ret_w_out[l].astype(BF16), xt)
        else:
            j = l - n_a
            q = norm_matmul(xt, ln_mix[l], att_w_q[j].astype(BF16), col_start=0, n_cols=d, out_dtype=BF16,
                            scale=ATT_HEAD_DIM ** -0.5)
            a = attention_core(q, kv, _attention_bias(att_rel_bias[j]), batch, seq)
            xt = matmul_residual(a, att_w_o[j].astype(BF16), xt)
        xt = peer_layer(xt, ln_ffn[l], peer_w_q[l].T.astype(BF16), peer_sub_keys[l].astype(BF16),
                        peer_u[l].astype(BF16), peer_v[l].T.astype(BF16))
        if l == n_a - 1:
            kv = norm_matmul(xt, kv_norm, w_kv.astype(BF16), col_start=0, n_cols=2 * d, out_dtype=BF16)
    return final_norm(xt, ln_final).reshape(batch, seq, d)
```

```python
import functools

import jax
import jax.numpy as jnp
from jax import lax
from jax.experimental import pallas as pl
from jax.experimental.pallas import tpu as pltpu

F32 = jnp.float32
BF16 = jnp.bfloat16

EPS = 1e-6
ROPE_BASE = 10000.0
REF_CHUNK = 64
RET_HEADS = 8
RET_QK_DIM = 256
RET_V_DIM = 512
RET_BLOCK = 256
ATT_HEADS = 16
ATT_HEAD_DIM = 128
LEFT_CHUNKS = 8
REL_CLIP = 128
ATT_QBLOCK = 256
ATT_WINDOW = ATT_QBLOCK + LEFT_CHUNKS * REF_CHUNK
PEER_HEADS = 8
PEER_NKEYS = 128
PEER_HALF = 128
PEER_TOPK = 16
MASK_VALUE = -1e30
NEG_INF = float("-inf")

V7X_LANES = 128
V7X_VMEM_LIMIT_BYTES = 60 * 1024 * 1024

NT_DIMS = (((1,), (1,)), ((), ()))
TN_DIMS = (((0,), (0,)), ((), ()))


def _params(*sem, flags=None):
    return pltpu.CompilerParams(dimension_semantics=sem, vmem_limit_bytes=V7X_VMEM_LIMIT_BYTES, flags=flags)


def _rmsnorm_rows(x, g):
    ms = jnp.mean(x * x, axis=-1, keepdims=True)
    return x * lax.rsqrt(ms + EPS) * g


def _norm_matmul_kernel(x_ref, g_ref, w_ref, o_ref, hn_ref, *, scale):
    @pl.when(pl.program_id(1) == 0)
    def _():
        hn_ref[...] = _rmsnorm_rows(x_ref[...], g_ref[...]).astype(BF16)

    acc = jnp.dot(hn_ref[...], w_ref[...], preferred_element_type=F32)
    if scale != 1.0:
        acc = acc * scale
    o_ref[...] = acc.astype(o_ref.dtype)


def norm_matmul(x, gain, w, *, col_start, n_cols, out_dtype, scale=1.0, tm=1024, tn=1024):
    t, d = x.shape
    tm = min(tm, t)
    tn = min(tn, n_cols)
    assert t % tm == 0 and n_cols % tn == 0 and col_start % tn == 0
    off = col_start // tn
    return pl.pallas_call(
        functools.partial(_norm_matmul_kernel, scale=scale),
        grid=(t // tm, n_cols // tn),
        in_specs=[
            pl.BlockSpec((tm, d), lambda i, j: (i, 0)),
            pl.BlockSpec((1, d), lambda i, j: (0, 0)),
            pl.BlockSpec((d, tn), lambda i, j: (0, j + off)),
        ],
        out_specs=pl.BlockSpec((tm, tn), lambda i, j: (i, j)),
        out_shape=jax.ShapeDtypeStruct((t, n_cols), out_dtype),
        scratch_shapes=[pltpu.VMEM((tm, d), BF16)],
        compiler_params=_params("parallel", "arbitrary"),
        name="norm_matmul",
    )(x, gain.reshape(1, d), w)


def _matmul_residual_kernel(y_ref, w_ref, x_ref, o_ref):
    o_ref[...] = x_ref[...] + jnp.dot(y_ref[...], w_ref[...], preferred_element_type=F32)


def matmul_residual(y, w, x, *, tm=1024, tn=512):
    t, k = y.shape
    n = w.shape[1]
    tm = min(tm, t)
    tn = min(tn, n)
    assert t % tm == 0 and n % tn == 0
    return pl.pallas_call(
        _matmul_residual_kernel,
        grid=(t // tm, n // tn),
        in_specs=[
            pl.BlockSpec((tm, k), lambda i, j: (i, 0)),
            pl.BlockSpec((k, tn), lambda i, j: (0, j)),
            pl.BlockSpec((tm, tn), lambda i, j: (i, j)),
        ],
        out_specs=pl.BlockSpec((tm, tn), lambda i, j: (i, j)),
        out_shape=jax.ShapeDtypeStruct((t, n), F32),
        compiler_params=_params("parallel", "arbitrary"),
        name="matmul_residual",
    )(y, w, x)


def _retention_tables(seq):
    c = RET_BLOCK
    log_gamma = jnp.log1p(-jnp.exp2(-5.0 - jnp.arange(RET_HEADS, dtype=F32)))
    idx = jnp.arange(c, dtype=F32)
    diff = idx[:, None] - idx[None, :]
    dmask = jnp.where(diff >= 0, jnp.exp(log_gamma[:, None, None] * jnp.maximum(diff, 0.0)), 0.0)
    xi = jnp.exp(log_gamma[:, None] * (idx[None, :] + 1.0))
    zeta = jnp.exp(log_gamma[:, None] * (c - 1.0 - idx[None, :]))
    cdecay = jnp.exp(log_gamma * c)
    xi = jnp.broadcast_to(xi[:, :, None], (RET_HEADS, c, V7X_LANES))
    zeta = jnp.broadcast_to(zeta[:, :, None], (RET_HEADS, c, V7X_LANES))
    cdecay = jnp.broadcast_to(cdecay[:, None, None], (RET_HEADS, 1, RET_V_DIM))
    half = RET_QK_DIM // 2
    inv = 1.0 / (ROPE_BASE ** (jnp.arange(half, dtype=F32) / half))
    ang = jnp.arange(seq, dtype=F32)[:, None] * inv[None, :]
    return dmask, xi, zeta, cdecay, jnp.cos(ang), jnp.sin(ang)


def _retention_kernel(q_ref, k_ref, v_ref, g_ref, cos_ref, sin_ref, dmask_ref, xi_ref, zeta_ref, cd_ref,
                      gn_ref, y_ref, state_ref):
    @pl.when(pl.program_id(2) == 0)
    def _():
        state_ref[...] = jnp.zeros_like(state_ref)

    cos = cos_ref[...]
    sin = sin_ref[...]
    half = RET_QK_DIM // 2

    def rotate(t):
        t1 = t[:, :half]
        t2 = t[:, half:]
        return jnp.concatenate([t1 * cos - t2 * sin, t1 * sin + t2 * cos], axis=-1)

    q = rotate(q_ref[...])
    k = rotate(k_ref[...]) * (RET_QK_DIM ** -0.5)
    qb = q.astype(BF16)
    kb = k.astype(BF16)
    v = v_ref[...]

    scores = lax.dot_general(qb, kb, NT_DIMS, preferred_element_type=F32) * dmask_ref[...]
    intra = jnp.dot(scores.astype(BF16), v, preferred_element_type=F32)
    state = state_ref[...]
    xi = jnp.concatenate([xi_ref[...]] * (RET_V_DIM // V7X_LANES), axis=-1)
    cross = jnp.dot(qb, state.astype(BF16), preferred_element_type=F32) * xi
    o = intra + cross

    zeta = jnp.concatenate([zeta_ref[...]] * (RET_QK_DIM // V7X_LANES), axis=-1)
    kz = (k * zeta).astype(BF16)
    state_ref[...] = state * cd_ref[...] + lax.dot_general(kz, v, TN_DIMS, preferred_element_type=F32)

    mu = jnp.mean(o, axis=-1, keepdims=True)
    d = o - mu
    var = jnp.mean(d * d, axis=-1, keepdims=True)
    y = d * lax.rsqrt(var + EPS) * gn_ref[...]
    g = g_ref[...]
    y_ref[...] = (g * jax.nn.sigmoid(g) * y).astype(y_ref.dtype)


def retention_core(qk, v, g, gn, batch, seq):
    c = RET_BLOCK
    nc = seq // c
    dmask, xi, zeta, cdecay, cos, sin = _retention_tables(seq)
    row = lambda b, h, n: (b * nc + n, h)
    per_head = lambda b, h, n: (h, 0, 0)
    return pl.pallas_call(
        _retention_kernel,
        grid=(batch, RET_HEADS, nc),
        in_specs=[
            pl.BlockSpec((c, RET_QK_DIM), row),
            pl.BlockSpec((c, RET_QK_DIM), lambda b, h, n: (b * nc + n, RET_HEADS + h)),
            pl.BlockSpec((c, RET_V_DIM), row),
            pl.BlockSpec((c, RET_V_DIM), row),
            pl.BlockSpec((c, RET_QK_DIM // 2), lambda b, h, n: (n, 0)),
            pl.BlockSpec((c, RET_QK_DIM // 2), lambda b, h, n: (n, 0)),
            pl.BlockSpec((None, c, c), per_head),
            pl.BlockSpec((None, c, V7X_LANES), per_head),
            pl.BlockSpec((None, c, V7X_LANES), per_head),
            pl.BlockSpec((None, 1, RET_V_DIM), per_head),
            pl.BlockSpec((1, RET_V_DIM), lambda b, h, n: (0, h)),
        ],
        out_specs=pl.BlockSpec((c, RET_V_DIM), row),
        out_shape=jax.ShapeDtypeStruct(v.shape, BF16),
        scratch_shapes=[pltpu.VMEM((RET_QK_DIM, RET_V_DIM), F32)],
        compiler_params=_params("parallel", "parallel", "arbitrary"),
        name="retention_core",
    )(qk, qk, v, g, cos, sin, dmask, xi, zeta, cdecay, gn.reshape(1, -1))


def _attention_bias(rel_bias):
    i = jnp.arange(ATT_QBLOCK)[:, None]
    m = jnp.arange(ATT_WINDOW)[None, :]
    period = ATT_QBLOCK + ATT_WINDOW - 1
    lag = jnp.arange(period)
    lag = jnp.where(lag < ATT_WINDOW, lag, lag - period)
    rel = jnp.clip(LEFT_CHUNKS * REF_CHUNK - lag, -REL_CLIP, REL_CLIP) + REL_CLIP
    first = rel_bias.astype(F32)[:, rel]
    heads = rel_bias.shape[0]
    skew = jnp.tile(first, (1, ATT_QBLOCK + 1))[:, :ATT_QBLOCK * (period - 1)]
    toeplitz = skew.reshape(heads, ATT_QBLOCK, period - 1)[:, :, :ATT_WINDOW]
    ci = i // REF_CHUNK
    cm = m // REF_CHUNK
    band = (cm >= ci) & (cm <= ci + LEFT_CHUNKS)
    return jnp.where(band[None], toeplitz, MASK_VALUE)


def _attention_kernel(q_ref, k0_ref, k1_ref, k2_ref, v0_ref, v1_ref, v2_ref, bias_ref, o_ref):
    qblk = pl.program_id(1)
    n_prev = LEFT_CHUNKS * REF_CHUNK // ATT_QBLOCK
    col = lax.broadcasted_iota(jnp.int32, (1, ATT_WINDOW), 1)
    start_mask = jnp.where(col >= (n_prev - qblk) * ATT_QBLOCK, 0.0, MASK_VALUE)
    k_refs = (k0_ref, k1_ref, k2_ref)
    v_refs = (v0_ref, v1_ref, v2_ref)

    def one_head(h, carry):
        cols = pl.ds(pl.multiple_of(h * ATT_HEAD_DIM, ATT_HEAD_DIM), ATT_HEAD_DIM)
        q = q_ref[:, cols]
        s = jnp.concatenate(
            [lax.dot_general(q, kr[:, cols], NT_DIMS, preferred_element_type=F32) for kr in k_refs], axis=-1)
        s = s + bias_ref[h] + start_mask
        m = jnp.max(s, axis=-1, keepdims=True)
        p = jnp.exp(s - m)
        p = (p / jnp.sum(p, axis=-1, keepdims=True)).astype(BF16)
        o = jnp.dot(p[:, :ATT_QBLOCK], v_refs[0][:, cols], preferred_element_type=F32)
        for w in range(1, len(v_refs)):
            o = o + jnp.dot(p[:, w * ATT_QBLOCK:(w + 1) * ATT_QBLOCK], v_refs[w][:, cols],
                            preferred_element_type=F32)
        o_ref[:, cols] = o.astype(o_ref.dtype)
        return carry

    lax.fori_loop(0, ATT_HEADS, one_head, 0)


def attention_core(q, kv, bias, batch, seq):
    t, d = q.shape
    nq = seq // ATT_QBLOCK
    n_prev = LEFT_CHUNKS * REF_CHUNK // ATT_QBLOCK
    assert ATT_WINDOW == (n_prev + 1) * ATT_QBLOCK

    def window(w, part):
        return pl.BlockSpec((ATT_QBLOCK, d), lambda b, n: (b * nq + jnp.maximum(n - n_prev + w, 0), part))

    return pl.pallas_call(
        _attention_kernel,
        grid=(batch, nq),
        in_specs=[pl.BlockSpec((ATT_QBLOCK, d), lambda b, n: (b * nq + n, 0))]
        + [window(w, 0) for w in range(n_prev + 1)]
        + [window(w, 1) for w in range(n_prev + 1)]
        + [pl.BlockSpec(bias.shape, lambda b, n: (0, 0, 0))],
        out_specs=pl.BlockSpec((ATT_QBLOCK, d), lambda b, n: (b * nq + n, 0)),
        out_shape=jax.ShapeDtypeStruct((t, d), BF16),
        compiler_params=_params("parallel", "arbitrary"),
        name="attention_core",
    )(q, kv, kv, kv, kv, kv, kv, bias)


PEER_RANKS = PEER_TOPK + 1
PEER_RANK_ROWS = 24
PEER_CAND_ROWS = PEER_RANK_ROWS + 8 * (PEER_RANKS - 1)


def _extract_max(work_ref):
    w = work_ref[...]
    m = jnp.max(w, axis=0, keepdims=True)
    work_ref[...] = jnp.where(w == m, NEG_INF, w)
    return m


def _peer_prep_kernel(x_ref, g_ref, wqt_ref, sk_ref, hnt_ref, b2_ref, e2_ref, n1_ref, c_ref,
                      qt_ref, s1_ref, s2_ref, work_ref, top_ref, cand_ref):
    hnt = _rmsnorm_rows(x_ref[...], g_ref[...]).T.astype(BF16)
    hnt_ref[...] = hnt
    qt_ref[...] = jnp.dot(wqt_ref[...], hnt, preferred_element_type=F32)
    top_ref[...] = jnp.full(top_ref.shape, NEG_INF, F32)

    def one_head(h, carry):
        for part in range(2):
            rows = pl.ds(pl.multiple_of(h * (2 * PEER_HALF) + part * PEER_HALF, PEER_HALF), PEER_HALF)
            s = jnp.dot(sk_ref[part], qt_ref[rows, :].astype(BF16), preferred_element_type=F32)
            (s1_ref, s2_ref)[part][...] = s
            work_ref[...] = s

            def pop(r, c2, part=part):
                top_ref[part, pl.ds(r, 1), :] = _extract_max(work_ref)
                return c2

            lax.fori_loop(0, PEER_RANKS, pop, 0)

        xs = top_ref[0]
        ys = top_ref[1]
        cand_ref[0:PEER_RANK_ROWS, :] = xs[0:1, :] + ys
        for i in range(1, PEER_RANKS):
            cand_ref[PEER_RANK_ROWS + 8 * (i - 1):PEER_RANK_ROWS + 8 * i, :] = xs[i:i + 1, :] + ys[0:8, :]
        best = xs[0:1, :] + ys[0:1, :]

        top_sums = [_extract_max(cand_ref) for _ in range(PEER_RANKS)]
        z = jnp.exp(top_sums[0] - best)
        for v in top_sums[1:PEER_TOPK]:
            z = z + jnp.exp(v - best)
        tau = 0.5 * (top_sums[PEER_TOPK - 1] + top_sums[PEER_TOPK])
        s1 = s1_ref[...]
        s2 = s2_ref[...]
        thr = tau - s1
        b2 = jnp.zeros_like(s2)
        n1 = jnp.zeros_like(s1)
        for r in range(PEER_RANKS):
            y = ys[r:r + 1, :]
            b2 = b2 + jnp.where(y > s2, 1.0, 0.0)
            n1 = n1 + jnp.where(y >= thr, 1.0, 0.0)
        n1_ref[h] = n1
        c_ref[h] = jnp.exp(s1 - xs[0:1, :]) * (0.5 / z)
        e2 = jnp.exp(s2 - ys[0:1, :])
        for lg in range(s2.shape[1] // V7X_LANES):
            lanes = slice(lg * V7X_LANES, (lg + 1) * V7X_LANES)
            b2_ref[h, lg] = b2[:, lanes].astype(BF16)
            e2_ref[h, lg] = e2[:, lanes].astype(BF16)
        return carry

    lax.fori_loop(0, PEER_HEADS, one_head, 0)


def peer_prep(x, gain, wq_t, sub_keys, *, tm=512):
    t, d = x.shape
    tm = min(tm, t)
    assert t % tm == 0 and tm % V7X_LANES == 0
    nq = wq_t.shape[0]
    fac = jax.ShapeDtypeStruct((PEER_HEADS, PEER_NKEYS, t), F32)
    fac_spec = pl.BlockSpec((PEER_HEADS, PEER_NKEYS, tm), lambda i: (0, 0, i))
    blocked = jax.ShapeDtypeStruct((PEER_HEADS, t // V7X_LANES, PEER_NKEYS, V7X_LANES), BF16)
    blocked_spec = pl.BlockSpec((PEER_HEADS, tm // V7X_LANES, PEER_NKEYS, V7X_LANES), lambda i: (0, i, 0, 0))
    return pl.pallas_call(
        _peer_prep_kernel,
        grid=(t // tm,),
        in_specs=[
            pl.BlockSpec((tm, d), lambda i: (i, 0)),
            pl.BlockSpec((1, d), lambda i: (0, 0)),
            pl.BlockSpec((nq, d), lambda i: (0, 0)),
            pl.BlockSpec(sub_keys.shape, lambda i: (0, 0, 0)),
        ],
        out_specs=[pl.BlockSpec((d, tm), lambda i: (0, i)), blocked_spec, blocked_spec, fac_spec, fac_spec],
        out_shape=[jax.ShapeDtypeStruct((d, t), BF16), blocked, blocked, fac, fac],
        scratch_shapes=[
            pltpu.VMEM((nq, tm), F32),
            pltpu.VMEM((PEER_NKEYS, tm), F32),
            pltpu.VMEM((PEER_NKEYS, tm), F32),
            pltpu.VMEM((PEER_NKEYS, tm), F32),
            pltpu.VMEM((2, PEER_RANK_ROWS, tm), F32),
            pltpu.VMEM((PEER_CAND_ROWS, tm), F32),
        ],
        compiler_params=_params("parallel"),
        name="peer_prep",
    )(x, gain.reshape(1, d), wq_t, sub_keys)


PEER_KEYS_PER_STEP = 8
PEER_ETILE = PEER_KEYS_PER_STEP * PEER_NKEYS
PEER_SUBROWS = 16


def _peer_main_kernel(hnt_ref, u_ref, vt_ref, b2_ref, e2_ref, n1_ref, c_ref, x_ref, o_ref,
                      acc_ref, at_ref, p_ref):
    j = pl.program_id(1)
    tm = hnt_ref.shape[1]
    lane_groups = tm // V7X_LANES

    @pl.when(j == 0)
    def _():
        acc_ref[...] = jnp.zeros_like(acc_ref)

    pre_all = jnp.dot(u_ref[...], hnt_ref[...], preferred_element_type=F32)
    for lg in range(lane_groups):
        at_ref[lg] = pre_all[:, lg * V7X_LANES:(lg + 1) * V7X_LANES]

    for ka in range(PEER_KEYS_PER_STEP):
        for lg in range(lane_groups):
            lanes = slice(lg * V7X_LANES, (lg + 1) * V7X_LANES)
            tile = (PEER_SUBROWS, V7X_LANES)
            n1 = [jnp.broadcast_to(n1_ref[h, ka:ka + 1, lanes], tile).astype(BF16) for h in range(PEER_HEADS)]
            cw = [jnp.broadcast_to(c_ref[h, ka:ka + 1, lanes], tile).astype(BF16) for h in range(PEER_HEADS)]
            for bg in range(PEER_NKEYS // PEER_SUBROWS):
                brows = slice(bg * PEER_SUBROWS, (bg + 1) * PEER_SUBROWS)
                w = None
                for h in range(PEER_HEADS):
                    term = jnp.where(b2_ref[h, lg, brows, :] < n1[h], e2_ref[h, lg, brows, :] * cw[h], 0.0)
                    w = term if w is None else w + term
                erows = slice(ka * PEER_NKEYS + bg * PEER_SUBROWS, ka * PEER_NKEYS + (bg + 1) * PEER_SUBROWS)
                pre = at_ref[lg, erows, :]
                act = pre * (1.0 + lax.erf(pre * (2.0 ** -0.5)))
                p_ref[lg, erows, :] = act.astype(BF16) * w

    p_all = jnp.concatenate([p_ref[lg] for lg in range(lane_groups)], axis=-1)
    acc_ref[...] += jnp.dot(vt_ref[...], p_all, preferred_element_type=F32)

    @pl.when(j == pl.num_programs(1) - 1)
    def _():
        o_ref[...] = x_ref[...] + acc_ref[...].T


def peer_main(hn_t, u, v_t, b2, e2, n1, c, x, *, tm=512):
    t, d = x.shape
    n_exp = u.shape[0]
    tm = min(tm, t)
    assert t % tm == 0 and tm % V7X_LANES == 0
    assert n_exp % PEER_ETILE == 0 and n_exp == PEER_NKEYS * PEER_NKEYS
    lane_groups = tm // V7X_LANES
    all_keys = pl.BlockSpec((PEER_HEADS, lane_groups, PEER_NKEYS, V7X_LANES), lambda i, j: (0, i, 0, 0))
    step_keys = pl.BlockSpec((PEER_HEADS, PEER_KEYS_PER_STEP, tm), lambda i, j: (0, j, i))
    return pl.pallas_call(
        _peer_main_kernel,
        grid=(t // tm, n_exp // PEER_ETILE),
        in_specs=[
            pl.BlockSpec((d, tm), lambda i, j: (0, i)),
            pl.BlockSpec((PEER_ETILE, d), lambda i, j: (j, 0)),
            pl.BlockSpec((d, PEER_ETILE), lambda i, j: (0, j)),
            all_keys, all_keys, step_keys, step_keys,
            pl.BlockSpec((tm, d), lambda i, j: (i, 0)),
        ],
        out_specs=pl.BlockSpec((tm, d), lambda i, j: (i, 0)),
        out_shape=jax.ShapeDtypeStruct((t, d), F32),
        scratch_shapes=[
            pltpu.VMEM((d, tm), F32),
            pltpu.VMEM((lane_groups, PEER_ETILE, V7X_LANES), F32),
            pltpu.VMEM((lane_groups, PEER_ETILE, V7X_LANES), BF16),
        ],
        compiler_params=_params("parallel", "arbitrary"),
        name="peer_main",
    )(hn_t, u, v_t, b2, e2, n1, c, x)


def peer_layer(x, gain, wq_t, sub_keys, u, v_t):
    hn_t, b2, e2, n1, c = peer_prep(x, gain, wq_t, sub_keys)
    return peer_main(hn_t, u, v_t, b2, e2, n1, c, x)


def _final_norm_kernel(x_ref, g_ref, o_ref):
    o_ref[...] = _rmsnorm_rows(x_ref[...], g_ref[...])


def final_norm(x, gain, *, tm=1024):
    t, d = x.shape
    tm = min(tm, t)
    return pl.pallas_call(
        _final_norm_kernel,
        grid=(t // tm,),
        in_specs=[pl.BlockSpec((tm, d), lambda i: (i, 0)), pl.BlockSpec((1, d), lambda i: (0, 0))],
        out_specs=pl.BlockSpec((tm, d), lambda i: (i, 0)),
        out_shape=jax.ShapeDtypeStruct((t, d), F32),
        compiler_params=_params("parallel"),
        name="final_norm",
    )(x, gain.reshape(1, d))


def kernel(x, ln_mix, ln_ffn, ret_w_in, ret_w_out, ret_gn, kv_norm, w_kv, att_w_q, att_w_o, att_rel_bias,
           peer_w_q, peer_sub_keys, peer_u, peer_v, ln_final):
    batch, seq, d = x.shape
    depth = ln_mix.shape[0]
    n_a = ret_w_in.shape[0]
    xt = x.reshape(batch * seq, d)
    qk_width = 2 * RET_HEADS * RET_QK_DIM
    v_width = RET_HEADS * RET_V_DIM
    kv = None
    for l in range(depth):
        if l < n_a:
            w_in = ret_w_in[l].astype(BF16)
            qk = norm_matmul(xt, ln_mix[l], w_in, col_start=0, n_cols=qk_width, out_dtype=F32)
            v = norm_matmul(xt, ln_mix[l], w_in, col_start=qk_width, n_cols=v_width, out_dtype=BF16)
            g = norm_matmul(xt, ln_mix[l], w_in, col_start=qk_width + v_width, n_cols=v_width, out_dtype=F32)
            y = retention_core(qk, v, g, ret_gn[l], batch, seq)
            xt = matmul_residual(y, ret_w_out[l].astype(BF16), xt)
        else:
            j = l - n_a
            q = norm_matmul(xt, ln_mix[l], att_w_q[j].astype(BF16), col_start=0, n_cols=d, out_dtype=BF16,
                            scale=ATT_HEAD_DIM ** -0.5)
            a = attention_core(q, kv, _attention_bias(att_rel_bias[j]), batch, seq)
            xt = matmul_residual(a, att_w_o[j].astype(BF16), xt)
        xt = peer_layer(xt, ln_ffn[l], peer_w_q[l].T.astype(BF16), peer_sub_keys[l].astype(BF16),
                        peer_u[l].astype(BF16), peer_v[l].T.astype(BF16))
        if l == n_a - 1:
            kv = norm_matmul(xt, kv_norm, w_kv.astype(BF16), col_start=0, n_cols=2 * d, out_dtype=BF16)
    return final_norm(xt, ln_final).reshape(batch, seq, d)
```

```python
import functools

import jax
import jax.numpy as jnp
from jax import lax
from jax.experimental import pallas as pl
from jax.experimental.pallas import tpu as pltpu

F32 = jnp.float32
BF16 = jnp.bfloat16

EPS = 1e-6
ROPE_BASE = 10000.0
REF_CHUNK = 64
RET_HEADS = 8
RET_QK_DIM = 256
RET_V_DIM = 512
RET_BLOCK = 256
ATT_HEADS = 16
ATT_HEAD_DIM = 128
LEFT_CHUNKS = 8
REL_CLIP = 128
ATT_QBLOCK = 256
ATT_WINDOW = ATT_QBLOCK + LEFT_CHUNKS * REF_CHUNK
PEER_HEADS = 8
PEER_NKEYS = 128
PEER_HALF = 128
PEER_TOPK = 16
MASK_VALUE = -1e30
NEG_INF = float("-inf")

V7X_LANES = 128
V7X_VMEM_LIMIT_BYTES = 60 * 1024 * 1024

NT_DIMS = (((1,), (1,)), ((), ()))
TN_DIMS = (((0,), (0,)), ((), ()))


def _params(*sem, flags=None):
    return pltpu.CompilerParams(dimension_semantics=sem, vmem_limit_bytes=V7X_VMEM_LIMIT_BYTES, flags=flags)


def _rmsnorm_rows(x, g):
    ms = jnp.mean(x * x, axis=-1, keepdims=True)
    return x * lax.rsqrt(ms + EPS) * g


def _norm_matmul_kernel(x_ref, g_ref, w_ref, o_ref, hn_ref, *, scale):
    @pl.when(pl.program_id(1) == 0)
    def _():
        hn_ref[...] = _rmsnorm_rows(x_ref[...], g_ref[...]).astype(BF16)

    acc = jnp.dot(hn_ref[...], w_ref[...], preferred_element_type=F32)
    if scale != 1.0:
        acc = acc * scale
    o_ref[...] = acc.astype(o_ref.dtype)


def norm_matmul(x, gain, w, *, col_start, n_cols, out_dtype, scale=1.0, tm=1024, tn=1024):
    t, d = x.shape
    tm = min(tm, t)
    tn = min(tn, n_cols)
    assert t % tm == 0 and n_cols % tn == 0 and col_start % tn == 0
    off = col_start // tn
    return pl.pallas_call(
        functools.partial(_norm_matmul_kernel, scale=scale),
        grid=(t // tm, n_cols // tn),
        in_specs=[
            pl.BlockSpec((tm, d), lambda i, j: (i, 0)),
            pl.BlockSpec((1, d), lambda i, j: (0, 0)),
            pl.BlockSpec((d, tn), lambda i, j: (0, j + off)),
        ],
        out_specs=pl.BlockSpec((tm, tn), lambda i, j: (i, j)),
        out_shape=jax.ShapeDtypeStruct((t, n_cols), out_dtype),
        scratch_shapes=[pltpu.VMEM((tm, d), BF16)],
        compiler_params=_params("parallel", "arbitrary"),
        name="norm_matmul",
    )(x, gain.reshape(1, d), w)


def _matmul_residual_kernel(y_ref, w_ref, x_ref, o_ref):
    o_ref[...] = x_ref[...] + jnp.dot(y_ref[...], w_ref[...], preferred_element_type=F32)


def matmul_residual(y, w, x, *, tm=1024, tn=512):
    t, k = y.shape
    n = w.shape[1]
    tm = min(tm, t)
    tn = min(tn, n)
    assert t % tm == 0 and n % tn == 0
    return pl.pallas_call(
        _matmul_residual_kernel,
        grid=(t // tm, n // tn),
        in_specs=[
            pl.BlockSpec((tm, k), lambda i, j: (i, 0)),
            pl.BlockSpec((k, tn), lambda i, j: (0, j)),
            pl.BlockSpec((tm, tn), lambda i, j: (i, j)),
        ],
        out_specs=pl.BlockSpec((tm, tn), lambda i, j: (i, j)),
        out_shape=jax.ShapeDtypeStruct((t, n), F32),
        compiler_params=_params("parallel", "arbitrary"),
        name="matmul_residual",
    )(y, w, x)


def _retention_tables(seq):
    c = RET_BLOCK
    log_gamma = jnp.log1p(-jnp.exp2(-5.0 - jnp.arange(RET_HEADS, dtype=F32)))
    idx = jnp.arange(c, dtype=F32)
    diff = idx[:, None] - idx[None, :]
    dmask = jnp.where(diff >= 0, jnp.exp(log_gamma[:, None, None] * jnp.maximum(diff, 0.0)), 0.0)
    xi = jnp.exp(log_gamma[:, None] * (idx[None, :] + 1.0))
    zeta = jnp.exp(log_gamma[:, None] * (c - 1.0 - idx[None, :]))
    cdecay = jnp.exp(log_gamma * c)
    xi = jnp.broadcast_to(xi[:, :, None], (RET_HEADS, c, V7X_LANES))
    zeta = jnp.broadcast_to(zeta[:, :, None], (RET_HEADS, c, V7X_LANES))
    cdecay = jnp.broadcast_to(cdecay[:, None, None], (RET_HEADS, 1, RET_V_DIM))
    half = RET_QK_DIM // 2
    inv = 1.0 / (ROPE_BASE ** (jnp.arange(half, dtype=F32) / half))
    ang = jnp.arange(seq, dtype=F32)[:, None] * inv[None, :]
    return dmask, xi, zeta, cdecay, jnp.cos(ang), jnp.sin(ang)


def _retention_kernel(q_ref, k_ref, v_ref, g_ref, cos_ref, sin_ref, dmask_ref, xi_ref, zeta_ref, cd_ref,
                      gn_ref, y_ref, state_ref):
    @pl.when(pl.program_id(2) == 0)
    def _():
        state_ref[...] = jnp.zeros_like(state_ref)

    cos = cos_ref[...]
    sin = sin_ref[...]
    half = RET_QK_DIM // 2

    def rotate(t):
        t1 = t[:, :half]
        t2 = t[:, half:]
        return jnp.concatenate([t1 * cos - t2 * sin, t1 * sin + t2 * cos], axis=-1)

    q = rotate(q_ref[...])
    k = rotate(k_ref[...]) * (RET_QK_DIM ** -0.5)
    qb = q.astype(BF16)
    kb = k.astype(BF16)
    v = v_ref[...]

    scores = lax.dot_general(qb, kb, NT_DIMS, preferred_element_type=F32) * dmask_ref[...]
    intra = jnp.dot(scores.astype(BF16), v, preferred_element_type=F32)
    state = state_ref[...]
    xi = jnp.concatenate([xi_ref[...]] * (RET_V_DIM // V7X_LANES), axis=-1)
    cross = jnp.dot(qb, state.astype(BF16), preferred_element_type=F32) * xi
    o = intra + cross

    zeta = jnp.concatenate([zeta_ref[...]] * (RET_QK_DIM // V7X_LANES), axis=-1)
    kz = (k * zeta).astype(BF16)
    state_ref[...] = state * cd_ref[...] + lax.dot_general(kz, v, TN_DIMS, preferred_element_type=F32)

    mu = jnp.mean(o, axis=-1, keepdims=True)
    d = o - mu
    var = jnp.mean(d * d, axis=-1, keepdims=True)
    y = d * lax.rsqrt(var + EPS) * gn_ref[...]
    g = g_ref[...]
    y_ref[...] = (g * jax.nn.sigmoid(g) * y).astype(y_ref.dtype)


def retention_core(qk, v, g, gn, batch, seq):
    c = RET_BLOCK
    nc = seq // c
    dmask, xi, zeta, cdecay, cos, sin = _retention_tables(seq)
    row = lambda b, h, n: (b * nc + n, h)
    per_head = lambda b, h, n: (h, 0, 0)
    return pl.pallas_call(
        _retention_kernel,
        grid=(batch, RET_HEADS, nc),
        in_specs=[
            pl.BlockSpec((c, RET_QK_DIM), row),
            pl.BlockSpec((c, RET_QK_DIM), lambda b, h, n: (b * nc + n, RET_HEADS + h)),
            pl.BlockSpec((c, RET_V_DIM), row),
            pl.BlockSpec((c, RET_V_DIM), row),
            pl.BlockSpec((c, RET_QK_DIM // 2), lambda b, h, n: (n, 0)),
            pl.BlockSpec((c, RET_QK_DIM // 2), lambda b, h, n: (n, 0)),
            pl.BlockSpec((None, c, c), per_head),
            pl.BlockSpec((None, c, V7X_LANES), per_head),
            pl.BlockSpec((None, c, V7X_LANES), per_head),
            pl.BlockSpec((None, 1, RET_V_DIM), per_head),
            pl.BlockSpec((1, RET_V_DIM), lambda b, h, n: (0, h)),
        ],
        out_specs=pl.BlockSpec((c, RET_V_DIM), row),
        out_shape=jax.ShapeDtypeStruct(v.shape, BF16),
        scratch_shapes=[pltpu.VMEM((RET_QK_DIM, RET_V_DIM), F32)],
        compiler_params=_params("parallel", "parallel", "arbitrary"),
        name="retention_core",
    )(qk, qk, v, g, cos, sin, dmask, xi, zeta, cdecay, gn.reshape(1, -1))


def _attention_bias(rel_bias):
    i = jnp.arange(ATT_QBLOCK)[:, None]
    m = jnp.arange(ATT_WINDOW)[None, :]
    period = ATT_QBLOCK + ATT_WINDOW - 1
    lag = jnp.arange(period)
    lag = jnp.where(lag < ATT_WINDOW, lag, lag - period)
    rel = jnp.clip(LEFT_CHUNKS * REF_CHUNK - lag, -REL_CLIP, REL_CLIP) + REL_CLIP
    first = rel_bias.astype(F32)[:, rel]
    heads = rel_bias.shape[0]
    skew = jnp.tile(first, (1, ATT_QBLOCK + 1))[:, :ATT_QBLOCK * (period - 1)]
    toeplitz = skew.reshape(heads, ATT_QBLOCK, period - 1)[:, :, :ATT_WINDOW]
    ci = i // REF_CHUNK
    cm = m // REF_CHUNK
    band = (cm >= ci) & (cm <= ci + LEFT_CHUNKS)
    return jnp.where(band[None], toeplitz, MASK_VALUE)


def _attention_kernel(q_ref, k0_ref, k1_ref, k2_ref, v0_ref, v1_ref, v2_ref, bias_ref, o_ref):
    qblk = pl.program_id(1)
    n_prev = LEFT_CHUNKS * REF_CHUNK // ATT_QBLOCK
    col = lax.broadcasted_iota(jnp.int32, (1, ATT_WINDOW), 1)
    start_mask = jnp.where(col >= (n_prev - qblk) * ATT_QBLOCK, 0.0, MASK_VALUE)
    k_refs = (k0_ref, k1_ref, k2_ref)
    v_refs = (v0_ref, v1_ref, v2_ref)

    def one_head(h, carry):
        cols = pl.ds(pl.multiple_of(h * ATT_HEAD_DIM, ATT_HEAD_DIM), ATT_HEAD_DIM)
        q = q_ref[:, cols]
        s = jnp.concatenate(
            [lax.dot_general(q, kr[:, cols], NT_DIMS, preferred_element_type=F32) for kr in k_refs], axis=-1)
        s = s + bias_ref[h] + start_mask
        m = jnp.max(s, axis=-1, keepdims=True)
        p = jnp.exp(s - m)
        p = (p / jnp.sum(p, axis=-1, keepdims=True)).astype(BF16)
        o = jnp.dot(p[:, :ATT_QBLOCK], v_refs[0][:, cols], preferred_element_type=F32)
        for w in range(1, len(v_refs)):
            o = o + jnp.dot(p[:, w * ATT_QBLOCK:(w + 1) * ATT_QBLOCK], v_refs[w][:, cols],
                            preferred_element_type=F32)
        o_ref[:, cols] = o.astype(o_ref.dtype)
        return carry

    lax.fori_loop(0, ATT_HEADS, one_head, 0)


def attention_core(q, kv, bias, batch, seq):
    t, d = q.shape
    nq = seq // ATT_QBLOCK
    n_prev = LEFT_CHUNKS * REF_CHUNK // ATT_QBLOCK
    assert ATT_WINDOW == (n_prev + 1) * ATT_QBLOCK

    def window(w, part):
        return pl.BlockSpec((ATT_QBLOCK, d), lambda b, n: (b * nq + jnp.maximum(n - n_prev + w, 0), part))

    return pl.pallas_call(
        _attention_kernel,
        grid=(batch, nq),
        in_specs=[pl.BlockSpec((ATT_QBLOCK, d), lambda b, n: (b * nq + n, 0))]
        + [window(w, 0) for w in range(n_prev + 1)]
        + [window(w, 1) for w in range(n_prev + 1)]
        + [pl.BlockSpec(bias.shape, lambda b, n: (0, 0, 0))],
        out_specs=pl.BlockSpec((ATT_QBLOCK, d), lambda b, n: (b * nq + n, 0)),
        out_shape=jax.ShapeDtypeStruct((t, d), BF16),
        compiler_params=_params("parallel", "arbitrary"),
        name="attention_core",
    )(q, kv, kv, kv, kv, kv, kv, bias)


PEER_RANKS = PEER_TOPK + 1
V7X_SUBLANES = 8


def _sorting_network(n):
    pairs = []
    p = 1
    while p < n:
        k = p
        while k >= 1:
            for j in range(k % p, n - k, 2 * k):
                for i in range(min(k, n - j - k)):
                    if (i + j) // (2 * p) == (i + j + k) // (2 * p):
                        pairs.append((i + j, i + j + k))
            k //= 2
        p *= 2
    return pairs


def _sort_descending(vals):
    vals = list(vals)
    for i, j in _sorting_network(len(vals)):
        vals[i], vals[j] = jnp.maximum(vals[i], vals[j]), jnp.minimum(vals[i], vals[j])
    return vals


def _top_ranked(s):
    n = PEER_TOPK
    tiles = [s[r * V7X_SUBLANES:(r + 1) * V7X_SUBLANES, :] for r in range(s.shape[0] // V7X_SUBLANES)]
    assert len(tiles) == n
    top = _sort_descending(tiles)
    shift = V7X_SUBLANES // 2
    while shift >= 1:
        other = [pltpu.roll(v, shift, 0) for v in top]
        top = [jnp.maximum(top[i], other[n - 1 - i]) for i in range(n)]
        k = n // 2
        while k >= 1:
            for i in range(n):
                if i & k == 0:
                    top[i], top[i + k] = jnp.maximum(top[i], top[i + k]), jnp.minimum(top[i], top[i + k])
            k //= 2
        shift //= 2
    below = [jnp.where(t < top[n - 1], t, NEG_INF) for t in tiles]
    while len(below) > 1:
        below = [jnp.maximum(below[2 * i], below[2 * i + 1]) for i in range(len(below) // 2)]
    nxt = below[0]
    shift = V7X_SUBLANES // 2
    while shift >= 1:
        nxt = jnp.maximum(nxt, pltpu.roll(nxt, shift, 0))
        shift //= 2
    return top + [nxt]


def _peer_prep_kernel(x_ref, g_ref, wqt_ref, sk_ref, hnt_ref, s2_ref, e2_ref, thr_ref, c_ref, qt_ref):
    tm = x_ref.shape[0]
    lane_groups = tm // V7X_LANES
    assert lane_groups <= V7X_SUBLANES
    hnt = _rmsnorm_rows(x_ref[...], g_ref[...]).T.astype(BF16)
    hnt_ref[...] = hnt
    qt_ref[...] = jnp.dot(wqt_ref[...], hnt, preferred_element_type=F32)
    sublane = lax.broadcasted_iota(jnp.int32, (V7X_SUBLANES, V7X_LANES), 0)

    def one_head(h, carry):
        scores = []
        ranked = []
        for part in range(2):
            rows = pl.ds(pl.multiple_of(h * (2 * PEER_HALF) + part * PEER_HALF, PEER_HALF), PEER_HALF)
            s = jnp.dot(sk_ref[part], qt_ref[rows, :].astype(BF16), preferred_element_type=F32)
            scores.append(s)
            per_group = [_top_ranked(s[:, g * V7X_LANES:(g + 1) * V7X_LANES]) for g in range(lane_groups)]
            dense = []
            for r in range(PEER_RANKS):
                tile = per_group[lane_groups - 1][r]
                for g in range(lane_groups - 2, -1, -1):
                    tile = jnp.where(sublane == g, per_group[g][r], tile)
                dense.append(tile)
            ranked.append(dense)

        xs, ys = ranked
        sums = [xs[i] + ys[j] for i in range(PEER_RANKS) for j in range(PEER_RANKS) if (i + 1) * (j + 1) <= PEER_RANKS]
        size = 1
        while size < len(sums):
            size *= 2
        sums = _sort_descending(sums + [jnp.full_like(sums[0], NEG_INF)] * (size - len(sums)))
        z = jnp.ones_like(sums[0])
        for v in sums[1:PEER_TOPK]:
            z = z + jnp.exp(v - sums[0])
        tau = 0.5 * (sums[PEER_TOPK - 1] + sums[PEER_TOPK])
        half_over_z = 0.5 / z

        s1, s2 = scores
        for g in range(lane_groups):
            lanes = slice(g * V7X_LANES, (g + 1) * V7X_LANES)
            row = lambda t, g=g: t[g:g + 1, :]
            thr_ref[h, :, lanes] = row(tau) - s1[:, lanes]
            c_ref[h, :, lanes] = jnp.exp(s1[:, lanes] - row(xs[0])) * row(half_over_z)
            s2_ref[h, g] = s2[:, lanes]
            e2_ref[h, g] = jnp.exp(s2[:, lanes] - row(ys[0]))
        return carry

    lax.fori_loop(0, PEER_HEADS, one_head, 0)


def peer_prep(x, gain, wq_t, sub_keys, *, tm=512):
    t, d = x.shape
    tm = min(tm, t)
    assert t % tm == 0 and tm % V7X_LANES == 0
    nq = wq_t.shape[0]
    fac = jax.ShapeDtypeStruct((PEER_HEADS, PEER_NKEYS, t), F32)
    fac_spec = pl.BlockSpec((PEER_HEADS, PEER_NKEYS, tm), lambda i: (0, 0, i))
    blocked = jax.ShapeDtypeStruct((PEER_HEADS, t // V7X_LANES, PEER_NKEYS, V7X_LANES), F32)
    blocked_spec = pl.BlockSpec((PEER_HEADS, tm // V7X_LANES, PEER_NKEYS, V7X_LANES), lambda i: (0, i, 0, 0))
    return pl.pallas_call(
        _peer_prep_kernel,
        grid=(t // tm,),
        in_specs=[
            pl.BlockSpec((tm, d), lambda i: (i, 0)),
            pl.BlockSpec((1, d), lambda i: (0, 0)),
            pl.BlockSpec((nq, d), lambda i: (0, 0)),
            pl.BlockSpec(sub_keys.shape, lambda i: (0, 0, 0)),
        ],
        out_specs=[pl.BlockSpec((d, tm), lambda i: (0, i)), blocked_spec, blocked_spec, fac_spec, fac_spec],
        out_shape=[jax.ShapeDtypeStruct((d, t), BF16), blocked, blocked, fac, fac],
        scratch_shapes=[pltpu.VMEM((nq, tm), F32)],
        compiler_params=_params("parallel"),
        name="peer_prep",
    )(x, gain.reshape(1, d), wq_t, sub_keys)


PEER_KEYS_PER_STEP = 8
PEER_ETILE = PEER_KEYS_PER_STEP * PEER_NKEYS
PEER_SUBROWS = 32


def _peer_main_kernel(hnt_ref, u_ref, vt_ref, s2_ref, e2_ref, thr_ref, c_ref, x_ref, o_ref,
                      acc_ref, at_ref, p_ref):
    j = pl.program_id(1)
    tm = hnt_ref.shape[1]
    lane_groups = tm // V7X_LANES

    @pl.when(j == 0)
    def _():
        acc_ref[...] = jnp.zeros_like(acc_ref)

    pre_all = jnp.dot(u_ref[...], hnt_ref[...], preferred_element_type=F32)
    for lg in range(lane_groups):
        at_ref[lg] = pre_all[:, lg * V7X_LANES:(lg + 1) * V7X_LANES]

    for ka in range(PEER_KEYS_PER_STEP):
        for lg in range(lane_groups):
            lanes = slice(lg * V7X_LANES, (lg + 1) * V7X_LANES)
            thr = [thr_ref[h, ka:ka + 1, lanes] for h in range(PEER_HEADS)]
            cw = [c_ref[h, ka:ka + 1, lanes] for h in range(PEER_HEADS)]
            for bg in range(PEER_NKEYS // PEER_SUBROWS):
                brows = slice(bg * PEER_SUBROWS, (bg + 1) * PEER_SUBROWS)
                w = None
                for h in range(PEER_HEADS):
                    term = jnp.where(s2_ref[h, lg, brows, :] >= thr[h], e2_ref[h, lg, brows, :] * cw[h], 0.0)
                    w = term if w is None else w + term
                erows = slice(ka * PEER_NKEYS + bg * PEER_SUBROWS, ka * PEER_NKEYS + (bg + 1) * PEER_SUBROWS)
                pre = at_ref[lg, erows, :]
                act = pre * (1.0 + lax.erf(pre * (2.0 ** -0.5)))
                p_ref[lg, erows, :] = (act * w).astype(BF16)

    p_all = jnp.concatenate([p_ref[lg] for lg in range(lane_groups)], axis=-1)
    acc_ref[...] += jnp.dot(vt_ref[...], p_all, preferred_element_type=F32)

    @pl.when(j == pl.num_programs(1) - 1)
    def _():
        o_ref[...] = x_ref[...] + acc_ref[...].T


def peer_main(hn_t, u, v_t, s2, e2, thr, c, x, *, tm=512):
    t, d = x.shape
    n_exp = u.shape[0]
    tm = min(tm, t)
    assert t % tm == 0 and tm % V7X_LANES == 0
    assert n_exp % PEER_ETILE == 0 and n_exp == PEER_NKEYS * PEER_NKEYS
    lane_groups = tm // V7X_LANES
    all_keys = pl.BlockSpec((PEER_HEADS, lane_groups, PEER_NKEYS, V7X_LANES), lambda i, j: (0, i, 0, 0))
    step_keys = pl.BlockSpec((PEER_HEADS, PEER_KEYS_PER_STEP, tm), lambda i, j: (0, j, i))
    return pl.pallas_call(
        _peer_main_kernel,
        grid=(t // tm, n_exp // PEER_ETILE),
        in_specs=[
            pl.BlockSpec((d, tm), lambda i, j: (0, i)),
            pl.BlockSpec((PEER_ETILE, d), lambda i, j: (j, 0)),
            pl.BlockSpec((d, PEER_ETILE), lambda i, j: (0, j)),
            all_keys, all_keys, step_keys, step_keys,
            pl.BlockSpec((tm, d), lambda i, j: (i, 0)),
        ],
        out_specs=pl.BlockSpec((tm, d), lambda i, j: (i, 0)),
        out_shape=jax.ShapeDtypeStruct((t, d), F32),
        scratch_shapes=[
            pltpu.VMEM((d, tm), F32),
            pltpu.VMEM((lane_groups, PEER_ETILE, V7X_LANES), F32),
            pltpu.VMEM((lane_groups, PEER_ETILE, V7X_LANES), BF16),
        ],
        compiler_params=_params("parallel", "arbitrary"),
        name="peer_main",
    )(hn_t, u, v_t, s2, e2, thr, c, x)


def peer_layer(x, gain, wq_t, sub_keys, u, v_t):
    hn_t, s2, e2, thr, c = peer_prep(x, gain, wq_t, sub_keys)
    return peer_main(hn_t, u, v_t, s2, e2, thr, c, x)


def _final_norm_kernel(x_ref, g_ref, o_ref):
    o_ref[...] = _rmsnorm_rows(x_ref[...], g_ref[...])


def final_norm(x, gain, *, tm=1024):
    t, d = x.shape
    tm = min(tm, t)
    return pl.pallas_call(
        _final_norm_kernel,
        grid=(t // tm,),
        in_specs=[pl.BlockSpec((tm, d), lambda i: (i, 0)), pl.BlockSpec((1, d), lambda i: (0, 0))],
        out_specs=pl.BlockSpec((tm, d), lambda i: (i, 0)),
        out_shape=jax.ShapeDtypeStruct((t, d), F32),
        compiler_params=_params("parallel"),
        name="final_norm",
    )(x, gain.reshape(1, d))


def kernel(x, ln_mix, ln_ffn, ret_w_in, ret_w_out, ret_gn, kv_norm, w_kv, att_w_q, att_w_o, att_rel_bias,
           peer_w_q, peer_sub_keys, peer_u, peer_v, ln_final):
    batch, seq, d = x.shape
    depth = ln_mix.shape[0]
    n_a = ret_w_in.shape[0]
    xt = x.reshape(batch * seq, d)
    qk_width = 2 * RET_HEADS * RET_QK_DIM
    v_width = RET_HEADS * RET_V_DIM
    kv = None
    for l in range(depth):
        if l < n_a:
            w_in = ret_w_in[l].astype(BF16)
            qk = norm_matmul(xt, ln_mix[l], w_in, col_start=0, n_cols=qk_width, out_dtype=F32)
            v = norm_matmul(xt, ln_mix[l], w_in, col_start=qk_width, n_cols=v_width, out_dtype=BF16)
            g = norm_matmul(xt, ln_mix[l], w_in, col_start=qk_width + v_width, n_cols=v_width, out_dtype=F32)
            y = retention_core(qk, v, g, ret_gn[l], batch, seq)
            xt = matmul_residual(y, ret_w_out[l].astype(BF16), xt)
        else:
            j = l - n_a
            q = norm_matmul(xt, ln_mix[l], att_w_q[j].astype(BF16), col_start=0, n_cols=d, out_dtype=BF16,
                            scale=ATT_HEAD_DIM ** -0.5)
            a = attention_core(q, kv, _attention_bias(att_rel_bias[j]), batch, seq)
            xt = matmul_residual(a, att_w_o[j].astype(BF16), xt)
        xt = peer_layer(xt, ln_ffn[l], peer_w_q[l].T.astype(BF16), peer_sub_keys[l].astype(BF16),
                        peer_u[l].astype(BF16), peer_v[l].T.astype(BF16))
        if l == n_a - 1:
            kv = norm_matmul(xt, kv_norm, w_kv.astype(BF16), col_start=0, n_cols=2 * d, out_dtype=BF16)
    return final_norm(xt, ln_final).reshape(batch, seq, d)
```

```python
import functools

import jax
import jax.numpy as jnp
from jax import lax
from jax.experimental import pallas as pl
from jax.experimental.pallas import tpu as pltpu

F32 = jnp.float32
BF16 = jnp.bfloat16

EPS = 1e-6
ROPE_BASE = 10000.0
REF_CHUNK = 64
RET_HEADS = 8
RET_QK_DIM = 256
RET_V_DIM = 512
RET_BLOCK = 512
ATT_HEADS = 16
ATT_HEAD_DIM = 128
LEFT_CHUNKS = 8
REL_CLIP = 128
ATT_QBLOCK = 256
ATT_WINDOW = ATT_QBLOCK + LEFT_CHUNKS * REF_CHUNK
ATT_HEAD_UNROLL = 4
PEER_HEADS = 8
PEER_NKEYS = 128
PEER_HALF = 128
PEER_TOPK = 16
MASK_VALUE = -1e30
NEG_INF = float("-inf")

V7X_LANES = 128
V7X_VMEM_LIMIT_BYTES = 60 * 1024 * 1024

NT_DIMS = (((1,), (1,)), ((), ()))
TN_DIMS = (((0,), (0,)), ((), ()))


def _params(*sem, flags=None):
    return pltpu.CompilerParams(dimension_semantics=sem, vmem_limit_bytes=V7X_VMEM_LIMIT_BYTES, flags=flags)


def _rmsnorm_rows(x, g):
    ms = jnp.mean(x * x, axis=-1, keepdims=True)
    return x * lax.rsqrt(ms + EPS) * g


def _norm_matmul_kernel(x_ref, g_ref, w_ref, o_ref, hn_ref, *, scale):
    @pl.when(pl.program_id(1) == 0)
    def _():
        hn_ref[...] = _rmsnorm_rows(x_ref[...], g_ref[...]).astype(BF16)

    acc = jnp.dot(hn_ref[...], w_ref[...], preferred_element_type=F32)
    if scale != 1.0:
        acc = acc * scale
    o_ref[...] = acc.astype(o_ref.dtype)


def norm_matmul(x, gain, w, *, col_start, n_cols, out_dtype, scale=1.0, tm=1024, tn=1024):
    t, d = x.shape
    tm = min(tm, t)
    tn = min(tn, n_cols)
    assert t % tm == 0 and n_cols % tn == 0 and col_start % tn == 0
    off = col_start // tn
    return pl.pallas_call(
        functools.partial(_norm_matmul_kernel, scale=scale),
        grid=(t // tm, n_cols // tn),
        in_specs=[
            pl.BlockSpec((tm, d), lambda i, j: (i, 0)),
            pl.BlockSpec((1, d), lambda i, j: (0, 0)),
            pl.BlockSpec((d, tn), lambda i, j: (0, j + off)),
        ],
        out_specs=pl.BlockSpec((tm, tn), lambda i, j: (i, j)),
        out_shape=jax.ShapeDtypeStruct((t, n_cols), out_dtype),
        scratch_shapes=[pltpu.VMEM((tm, d), BF16)],
        compiler_params=_params("parallel", "arbitrary"),
        name="norm_matmul",
    )(x, gain.reshape(1, d), w)


def _matmul_residual_kernel(y_ref, w_ref, x_ref, o_ref):
    o_ref[...] = x_ref[...] + jnp.dot(y_ref[...], w_ref[...], preferred_element_type=F32)


def matmul_residual(y, w, x, *, tm=1024, tn=512):
    t, k = y.shape
    n = w.shape[1]
    tm = min(tm, t)
    tn = min(tn, n)
    assert t % tm == 0 and n % tn == 0
    return pl.pallas_call(
        _matmul_residual_kernel,
        grid=(t // tm, n // tn),
        in_specs=[
            pl.BlockSpec((tm, k), lambda i, j: (i, 0)),
            pl.BlockSpec((k, tn), lambda i, j: (0, j)),
            pl.BlockSpec((tm, tn), lambda i, j: (i, j)),
        ],
        out_specs=pl.BlockSpec((tm, tn), lambda i, j: (i, j)),
        out_shape=jax.ShapeDtypeStruct((t, n), F32),
        compiler_params=_params("parallel", "arbitrary"),
        name="matmul_residual",
    )(y, w, x)


def _retention_tables(seq):
    c = RET_BLOCK
    log_gamma = jnp.log1p(-jnp.exp2(-5.0 - jnp.arange(RET_HEADS, dtype=F32)))
    idx = jnp.arange(c, dtype=F32)
    diff = idx[:, None] - idx[None, :]
    dmask = jnp.where(diff >= 0, jnp.exp(log_gamma[:, None, None] * jnp.maximum(diff, 0.0)), 0.0)
    xi = jnp.exp(log_gamma[:, None] * (idx[None, :] + 1.0))
    zeta = jnp.exp(log_gamma[:, None] * (c - 1.0 - idx[None, :]))
    cdecay = jnp.exp(log_gamma * c)
    xi = jnp.broadcast_to(xi[:, :, None], (RET_HEADS, c, V7X_LANES))
    zeta = jnp.broadcast_to(zeta[:, :, None], (RET_HEADS, c, V7X_LANES))
    cdecay = jnp.broadcast_to(cdecay[:, None, None], (RET_HEADS, 1, RET_V_DIM))
    half = RET_QK_DIM // 2
    inv = 1.0 / (ROPE_BASE ** (jnp.arange(half, dtype=F32) / half))
    ang = jnp.arange(seq, dtype=F32)[:, None] * inv[None, :]
    return dmask, xi, zeta, cdecay, jnp.cos(ang), jnp.sin(ang)


def _retention_kernel(q_ref, k_ref, v_ref, g_ref, cos_ref, sin_ref, dmask_ref, xi_ref, zeta_ref, cd_ref,
                      gn_ref, y_ref, state_ref):
    @pl.when(pl.program_id(2) == 0)
    def _():
        state_ref[...] = jnp.zeros_like(state_ref)

    cos = cos_ref[...]
    sin = sin_ref[...]
    half = RET_QK_DIM // 2

    def rotate(t):
        t1 = t[:, :half]
        t2 = t[:, half:]
        return jnp.concatenate([t1 * cos - t2 * sin, t1 * sin + t2 * cos], axis=-1)

    q = rotate(q_ref[...])
    k = rotate(k_ref[...]) * (RET_QK_DIM ** -0.5)
    qb = q.astype(BF16)
    kb = k.astype(BF16)
    v = v_ref[...]

    scores = lax.dot_general(qb, kb, NT_DIMS, preferred_element_type=F32) * dmask_ref[...]
    intra = jnp.dot(scores.astype(BF16), v, preferred_element_type=F32)
    state = state_ref[...]
    xi = jnp.concatenate([xi_ref[...]] * (RET_V_DIM // V7X_LANES), axis=-1)
    cross = jnp.dot(qb, state.astype(BF16), preferred_element_type=F32) * xi
    o = intra + cross

    zeta = jnp.concatenate([zeta_ref[...]] * (RET_QK_DIM // V7X_LANES), axis=-1)
    kz = (k * zeta).astype(BF16)
    state_ref[...] = state * cd_ref[...] + lax.dot_general(kz, v, TN_DIMS, preferred_element_type=F32)

    mu = jnp.mean(o, axis=-1, keepdims=True)
    d = o - mu
    var = jnp.mean(d * d, axis=-1, keepdims=True)
    y = d * lax.rsqrt(var + EPS) * gn_ref[...]
    g = g_ref[...]
    y_ref[...] = (g * jax.nn.sigmoid(g) * y).astype(y_ref.dtype)


def retention_core(qk, v, g, gn, batch, seq):
    c = RET_BLOCK
    nc = seq // c
    dmask, xi, zeta, cdecay, cos, sin = _retention_tables(seq)
    row = lambda b, h, n: (b * nc + n, h)
    per_head = lambda b, h, n: (h, 0, 0)
    return pl.pallas_call(
        _retention_kernel,
        grid=(batch, RET_HEADS, nc),
        in_specs=[
            pl.BlockSpec((c, RET_QK_DIM), row),
            pl.BlockSpec((c, RET_QK_DIM), lambda b, h, n: (b * nc + n, RET_HEADS + h)),
            pl.BlockSpec((c, RET_V_DIM), row),
            pl.BlockSpec((c, RET_V_DIM), row),
            pl.BlockSpec((c, RET_QK_DIM // 2), lambda b, h, n: (n, 0)),
            pl.BlockSpec((c, RET_QK_DIM // 2), lambda b, h, n: (n, 0)),
            pl.BlockSpec((None, c, c), per_head),
            pl.BlockSpec((None, c, V7X_LANES), per_head),
            pl.BlockSpec((None, c, V7X_LANES), per_head),
            pl.BlockSpec((None, 1, RET_V_DIM), per_head),
            pl.BlockSpec((1, RET_V_DIM), lambda b, h, n: (0, h)),
        ],
        out_specs=pl.BlockSpec((c, RET_V_DIM), row),
        out_shape=jax.ShapeDtypeStruct(v.shape, BF16),
        scratch_shapes=[pltpu.VMEM((RET_QK_DIM, RET_V_DIM), F32)],
        compiler_params=_params("parallel", "parallel", "arbitrary"),
        name="retention_core",
    )(qk, qk, v, g, cos, sin, dmask, xi, zeta, cdecay, gn.reshape(1, -1))


def _attention_bias(rel_bias):
    i = jnp.arange(ATT_QBLOCK)[:, None]
    m = jnp.arange(ATT_WINDOW)[None, :]
    period = ATT_QBLOCK + ATT_WINDOW - 1
    lag = jnp.arange(period)
    lag = jnp.where(lag < ATT_WINDOW, lag, lag - period)
    rel = jnp.clip(LEFT_CHUNKS * REF_CHUNK - lag, -REL_CLIP, REL_CLIP) + REL_CLIP
    first = rel_bias.astype(F32)[:, rel]
    heads = rel_bias.shape[0]
    skew = jnp.tile(first, (1, ATT_QBLOCK + 1))[:, :ATT_QBLOCK * (period - 1)]
    toeplitz = skew.reshape(heads, ATT_QBLOCK, period - 1)[:, :, :ATT_WINDOW]
    ci = i // REF_CHUNK
    cm = m // REF_CHUNK
    band = (cm >= ci) & (cm <= ci + LEFT_CHUNKS)
    return jnp.where(band[None], toeplitz, MASK_VALUE)


def _attention_kernel(q_ref, k0_ref, k1_ref, k2_ref, v0_ref, v1_ref, v2_ref, bias_ref, o_ref):
    qblk = pl.program_id(1)
    n_prev = LEFT_CHUNKS * REF_CHUNK // ATT_QBLOCK
    col = lax.broadcasted_iota(jnp.int32, (1, ATT_WINDOW), 1)
    start_mask = jnp.where(col >= (n_prev - qblk) * ATT_QBLOCK, 0.0, MASK_VALUE)
    k_refs = (k0_ref, k1_ref, k2_ref)
    v_refs = (v0_ref, v1_ref, v2_ref)

    def one_head(h, carry):
        cols = pl.ds(pl.multiple_of(h * ATT_HEAD_DIM, ATT_HEAD_DIM), ATT_HEAD_DIM)
        q = q_ref[:, cols]
        s = jnp.concatenate(
            [lax.dot_general(q, kr[:, cols], NT_DIMS, preferred_element_type=F32) for kr in k_refs], axis=-1)
        s = s + bias_ref[h] + start_mask
        m = jnp.max(s, axis=-1, keepdims=True)
        p = jnp.exp(s - m)
        p = (p / jnp.sum(p, axis=-1, keepdims=True)).astype(BF16)
        o = jnp.dot(p[:, :ATT_QBLOCK], v_refs[0][:, cols], preferred_element_type=F32)
        for w in range(1, len(v_refs)):
            o = o + jnp.dot(p[:, w * ATT_QBLOCK:(w + 1) * ATT_QBLOCK], v_refs[w][:, cols],
                            preferred_element_type=F32)
        o_ref[:, cols] = o.astype(o_ref.dtype)
        return carry

    lax.fori_loop(0, ATT_HEADS, one_head, 0, unroll=ATT_HEAD_UNROLL)


def attention_core(q, kv, bias, batch, seq):
    t, d = q.shape
    nq = seq // ATT_QBLOCK
    n_prev = LEFT_CHUNKS * REF_CHUNK // ATT_QBLOCK
    assert ATT_WINDOW == (n_prev + 1) * ATT_QBLOCK

    def window(w, part):
        return pl.BlockSpec((ATT_QBLOCK, d), lambda b, n: (b * nq + jnp.maximum(n - n_prev + w, 0), part))

    return pl.pallas_call(
        _attention_kernel,
        grid=(batch, nq),
        in_specs=[pl.BlockSpec((ATT_QBLOCK, d), lambda b, n: (b * nq + n, 0))]
        + [window(w, 0) for w in range(n_prev + 1)]
        + [window(w, 1) for w in range(n_prev + 1)]
        + [pl.BlockSpec(bias.shape, lambda b, n: (0, 0, 0))],
        out_specs=pl.BlockSpec((ATT_QBLOCK, d), lambda b, n: (b * nq + n, 0)),
        out_shape=jax.ShapeDtypeStruct((t, d), BF16),
        compiler_params=_params("parallel", "arbitrary"),
        name="attention_core",
    )(q, kv, kv, kv, kv, kv, kv, bias)


PEER_RANKS = PEER_TOPK + 1
V7X_SUBLANES = 8


def _sorting_network(n):
    pairs = []
    p = 1
    while p < n:
        k = p
        while k >= 1:
            for j in range(k % p, n - k, 2 * k):
                for i in range(min(k, n - j - k)):
                    if (i + j) // (2 * p) == (i + j + k) // (2 * p):
                        pairs.append((i + j, i + j + k))
            k //= 2
        p *= 2
    return pairs


def _sort_descending(vals):
    vals = list(vals)
    for i, j in _sorting_network(len(vals)):
        vals[i], vals[j] = jnp.maximum(vals[i], vals[j]), jnp.minimum(vals[i], vals[j])
    return vals


def _top_ranked(s):
    n = PEER_TOPK
    tiles = [s[r * V7X_SUBLANES:(r + 1) * V7X_SUBLANES, :] for r in range(s.shape[0] // V7X_SUBLANES)]
    assert len(tiles) == n
    top = _sort_descending(tiles)
    shift = V7X_SUBLANES // 2
    while shift >= 1:
        other = [pltpu.roll(v, shift, 0) for v in top]
        top = [jnp.maximum(top[i], other[n - 1 - i]) for i in range(n)]
        k = n // 2
        while k >= 1:
            for i in range(n):
                if i & k == 0:
                    top[i], top[i + k] = jnp.maximum(top[i], top[i + k]), jnp.minimum(top[i], top[i + k])
            k //= 2
        shift //= 2
    below = [jnp.where(t < top[n - 1], t, NEG_INF) for t in tiles]
    while len(below) > 1:
        below = [jnp.maximum(below[2 * i], below[2 * i + 1]) for i in range(len(below) // 2)]
    nxt = below[0]
    shift = V7X_SUBLANES // 2
    while shift >= 1:
        nxt = jnp.maximum(nxt, pltpu.roll(nxt, shift, 0))
        shift //= 2
    return top + [nxt]


def _peer_prep_kernel(x_ref, g_ref, wqt_ref, sk_ref, hnt_ref, b2_ref, e2_ref, n1_ref, c_ref, qt_ref):
    tm = x_ref.shape[0]
    lane_groups = tm // V7X_LANES
    assert lane_groups <= V7X_SUBLANES
    hnt = _rmsnorm_rows(x_ref[...], g_ref[...]).T.astype(BF16)
    hnt_ref[...] = hnt
    qt_ref[...] = jnp.dot(wqt_ref[...], hnt, preferred_element_type=F32)
    sublane = lax.broadcasted_iota(jnp.int32, (V7X_SUBLANES, V7X_LANES), 0)

    def one_head(h, carry):
        scores = []
        ranked = []
        for part in range(2):
            rows = pl.ds(pl.multiple_of(h * (2 * PEER_HALF) + part * PEER_HALF, PEER_HALF), PEER_HALF)
            s = jnp.dot(sk_ref[part], qt_ref[rows, :].astype(BF16), preferred_element_type=F32)
            scores.append(s)
            per_group = [_top_ranked(s[:, g * V7X_LANES:(g + 1) * V7X_LANES]) for g in range(lane_groups)]
            dense = []
            for r in range(PEER_RANKS):
                tile = per_group[lane_groups - 1][r]
                for g in range(lane_groups - 2, -1, -1):
                    tile = jnp.where(sublane == g, per_group[g][r], tile)
                dense.append(tile)
            ranked.append(dense)

        xs, ys = ranked
        sums = [xs[i] + ys[j] for i in range(PEER_RANKS) for j in range(PEER_RANKS) if (i + 1) * (j + 1) <= PEER_RANKS]
        size = 1
        while size < len(sums):
            size *= 2
        sums = _sort_descending(sums + [jnp.full_like(sums[0], NEG_INF)] * (size - len(sums)))
        z = jnp.ones_like(sums[0])
        for v in sums[1:PEER_TOPK]:
            z = z + jnp.exp(v - sums[0])
        tau = 0.5 * (sums[PEER_TOPK - 1] + sums[PEER_TOPK])
        half_over_z = 0.5 / z

        s1, s2 = scores
        for g in range(lane_groups):
            lanes = slice(g * V7X_LANES, (g + 1) * V7X_LANES)
            row = lambda t, g=g: t[g:g + 1, :]
            s1_g = s1[:, lanes]
            s2_g = s2[:, lanes]
            thr = row(tau) - s1_g
            b2 = jnp.zeros_like(s2_g)
            n1 = jnp.zeros_like(s1_g)
            for r in range(PEER_TOPK):
                y = row(ys[r])
                b2 = b2 + jnp.where(y > s2_g, 1.0, 0.0)
                n1 = n1 + jnp.where(y >= thr, 1.0, 0.0)
            n1_ref[h, :, lanes] = n1
            c_ref[h, :, lanes] = jnp.exp(s1_g - row(xs[0])) * row(half_over_z)
            b2_ref[h, g] = b2.astype(BF16)
            e2_ref[h, g] = jnp.exp(s2_g - row(ys[0])).astype(BF16)
        return carry

    lax.fori_loop(0, PEER_HEADS, one_head, 0)


def peer_prep(x, gain, wq_t, sub_keys, *, tm=512):
    t, d = x.shape
    tm = min(tm, t)
    assert t % tm == 0 and tm % V7X_LANES == 0
    nq = wq_t.shape[0]
    fac = jax.ShapeDtypeStruct((PEER_HEADS, PEER_NKEYS, t), F32)
    fac_spec = pl.BlockSpec((PEER_HEADS, PEER_NKEYS, tm), lambda i: (0, 0, i))
    blocked = jax.ShapeDtypeStruct((PEER_HEADS, t // V7X_LANES, PEER_NKEYS, V7X_LANES), BF16)
    blocked_spec = pl.BlockSpec((PEER_HEADS, tm // V7X_LANES, PEER_NKEYS, V7X_LANES), lambda i: (0, i, 0, 0))
    return pl.pallas_call(
        _peer_prep_kernel,
        grid=(t // tm,),
        in_specs=[
            pl.BlockSpec((tm, d), lambda i: (i, 0)),
            pl.BlockSpec((1, d), lambda i: (0, 0)),
            pl.BlockSpec((nq, d), lambda i: (0, 0)),
            pl.BlockSpec(sub_keys.shape, lambda i: (0, 0, 0)),
        ],
        out_specs=[pl.BlockSpec((d, tm), lambda i: (0, i)), blocked_spec, blocked_spec, fac_spec, fac_spec],
        out_shape=[jax.ShapeDtypeStruct((d, t), BF16), blocked, blocked, fac, fac],
        scratch_shapes=[pltpu.VMEM((nq, tm), F32)],
        compiler_params=_params("parallel"),
        name="peer_prep",
    )(x, gain.reshape(1, d), wq_t, sub_keys)


PEER_KEYS_PER_STEP = 8
PEER_ETILE = PEER_KEYS_PER_STEP * PEER_NKEYS
PEER_SUBROWS = 16


def _peer_main_kernel(hnt_ref, u_ref, vt_ref, b2_ref, e2_ref, n1_ref, c_ref, x_ref, o_ref,
                      acc_ref, at_ref, p_ref):
    j = pl.program_id(1)
    tm = hnt_ref.shape[1]
    lane_groups = tm // V7X_LANES

    @pl.when(j == 0)
    def _():
        acc_ref[...] = jnp.zeros_like(acc_ref)

    pre_all = jnp.dot(u_ref[...], hnt_ref[...], preferred_element_type=F32)
    for lg in range(lane_groups):
        at_ref[lg] = pre_all[:, lg * V7X_LANES:(lg + 1) * V7X_LANES]

    for ka in range(PEER_KEYS_PER_STEP):
        for lg in range(lane_groups):
            lanes = slice(lg * V7X_LANES, (lg + 1) * V7X_LANES)
            tile = (PEER_SUBROWS, V7X_LANES)
            n1 = [jnp.broadcast_to(n1_ref[h, ka:ka + 1, lanes], tile).astype(BF16) for h in range(PEER_HEADS)]
            cw = [jnp.broadcast_to(c_ref[h, ka:ka + 1, lanes], tile).astype(BF16) for h in range(PEER_HEADS)]
            for bg in range(PEER_NKEYS // PEER_SUBROWS):
                brows = slice(bg * PEER_SUBROWS, (bg + 1) * PEER_SUBROWS)
                w = None
                for h in range(PEER_HEADS):
                    term = jnp.where(b2_ref[h, lg, brows, :] < n1[h], e2_ref[h, lg, brows, :] * cw[h], 0.0)
                    w = term if w is None else w + term
                erows = slice(ka * PEER_NKEYS + bg * PEER_SUBROWS, ka * PEER_NKEYS + (bg + 1) * PEER_SUBROWS)
                pre = at_ref[lg, erows, :]
                act = pre * (1.0 + lax.erf(pre * (2.0 ** -0.5)))
                p_ref[lg, erows, :] = act.astype(BF16) * w

    p_all = jnp.concatenate([p_ref[lg] for lg in range(lane_groups)], axis=-1)
    acc_ref[...] += jnp.dot(vt_ref[...], p_all, preferred_element_type=F32)

    @pl.when(j == pl.num_programs(1) - 1)
    def _():
        o_ref[...] = x_ref[...] + acc_ref[...].T


def peer_main(hn_t, u, v_t, b2, e2, n1, c, x, *, tm=512):
    t, d = x.shape
    n_exp = u.shape[0]
    tm = min(tm, t)
    assert t % tm == 0 and tm % V7X_LANES == 0
    assert n_exp % PEER_ETILE == 0 and n_exp == PEER_NKEYS * PEER_NKEYS
    lane_groups = tm // V7X_LANES
    all_keys = pl.BlockSpec((PEER_HEADS, lane_groups, PEER_NKEYS, V7X_LANES), lambda i, j: (0, i, 0, 0))
    step_keys = pl.BlockSpec((PEER_HEADS, PEER_KEYS_PER_STEP, tm), lambda i, j: (0, j, i))
    return pl.pallas_call(
        _peer_main_kernel,
        grid=(t // tm, n_exp // PEER_ETILE),
        in_specs=[
            pl.BlockSpec((d, tm), lambda i, j: (0, i)),
            pl.BlockSpec((PEER_ETILE, d), lambda i, j: (j, 0)),
            pl.BlockSpec((d, PEER_ETILE), lambda i, j: (0, j)),
            all_keys, all_keys, step_keys, step_keys,
            pl.BlockSpec((tm, d), lambda i, j: (i, 0)),
        ],
        out_specs=pl.BlockSpec((tm, d), lambda i, j: (i, 0)),
        out_shape=jax.ShapeDtypeStruct((t, d), F32),
        scratch_shapes=[
            pltpu.VMEM((d, tm), F32),
            pltpu.VMEM((lane_groups, PEER_ETILE, V7X_LANES), F32),
            pltpu.VMEM((lane_groups, PEER_ETILE, V7X_LANES), BF16),
        ],
        compiler_params=_params("parallel", "arbitrary"),
        name="peer_main",
    )(hn_t, u, v_t, b2, e2, n1, c, x)


def peer_layer(x, gain, wq_t, sub_keys, u, v_t):
    hn_t, b2, e2, n1, c = peer_prep(x, gain, wq_t, sub_keys)
    return peer_main(hn_t, u, v_t, b2, e2, n1, c, x)


def _final_norm_kernel(x_ref, g_ref, o_ref):
    o_ref[...] = _rmsnorm_rows(x_ref[...], g_ref[...])


def final_norm(x, gain, *, tm=1024):
    t, d = x.shape
    tm = min(tm, t)
    return pl.pallas_call(
        _final_norm_kernel,
        grid=(t // tm,),
        in_specs=[pl.BlockSpec((tm, d), lambda i: (i, 0)), pl.BlockSpec((1, d), lambda i: (0, 0))],
        out_specs=pl.BlockSpec((tm, d), lambda i: (i, 0)),
        out_shape=jax.ShapeDtypeStruct((t, d), F32),
        compiler_params=_params("parallel"),
        name="final_norm",
    )(x, gain.reshape(1, d))


def kernel(x, ln_mix, ln_ffn, ret_w_in, ret_w_out, ret_gn, kv_norm, w_kv, att_w_q, att_w_o, att_rel_bias,
           peer_w_q, peer_sub_keys, peer_u, peer_v, ln_final):
    batch, seq, d = x.shape
    depth = ln_mix.shape[0]
    n_a = ret_w_in.shape[0]
    xt = x.reshape(batch * seq, d)
    qk_width = 2 * RET_HEADS * RET_QK_DIM
    v_width = RET_HEADS * RET_V_DIM
    kv = None
    for l in range(depth):
        if l < n_a:
            w_in = ret_w_in[l].astype(BF16)
            qk = norm_matmul(xt, ln_mix[l], w_in, col_start=0, n_cols=qk_width, out_dtype=F32)
            v = norm_matmul(xt, ln_mix[l], w_in, col_start=qk_width, n_cols=v_width, out_dtype=BF16)
            g = norm_matmul(xt, ln_mix[l], w_in, col_start=qk_width + v_width, n_cols=v_width, out_dtype=F32)
            y = retention_core(qk, v, g, ret_gn[l], batch, seq)
            xt = matmul_residual(y, ret_w_out[l].astype(BF16), xt)
        else:
            j = l - n_a
            q = norm_matmul(xt, ln_mix[l], att_w_q[j].astype(BF16), col_start=0, n_cols=d, out_dtype=BF16,
                            scale=ATT_HEAD_DIM ** -0.5)
            a = attention_core(q, kv, _attention_bias(att_rel_bias[j]), batch, seq)
            xt = matmul_residual(a, att_w_o[j].astype(BF16), xt)
        xt = peer_layer(xt, ln_ffn[l], peer_w_q[l].T.astype(BF16), peer_sub_keys[l].astype(BF16),
                        peer_u[l].astype(BF16), peer_v[l].T.astype(BF16))
        if l == n_a - 1:
            kv = norm_matmul(xt, kv_norm, w_kv.astype(BF16), col_start=0, n_cols=2 * d, out_dtype=BF16)
    return final_norm(xt, ln_final).reshape(batch, seq, d)
```

```python
import functools

import jax
import jax.numpy as jnp
from jax import lax
from jax.experimental import pallas as pl
from jax.experimental.pallas import tpu as pltpu

F32 = jnp.float32
BF16 = jnp.bfloat16

EPS = 1e-6
ROPE_BASE = 10000.0
REF_CHUNK = 64
RET_HEADS = 8
RET_QK_DIM = 256
RET_V_DIM = 512
RET_BLOCK = 512
ATT_HEADS = 16
ATT_HEAD_DIM = 128
LEFT_CHUNKS = 8
REL_CLIP = 128
ATT_QBLOCK = 256
ATT_WINDOW = ATT_QBLOCK + LEFT_CHUNKS * REF_CHUNK
ATT_HEAD_UNROLL = 4
PEER_HEADS = 8
PEER_NKEYS = 128
PEER_HALF = 128
PEER_TOPK = 16
MASK_VALUE = -1e30
NEG_INF = float("-inf")

V7X_LANES = 128
V7X_VMEM_LIMIT_BYTES = 60 * 1024 * 1024

NT_DIMS = (((1,), (1,)), ((), ()))
TN_DIMS = (((0,), (0,)), ((), ()))


def _params(*sem, flags=None):
    return pltpu.CompilerParams(dimension_semantics=sem, vmem_limit_bytes=V7X_VMEM_LIMIT_BYTES, flags=flags)


def _rmsnorm_rows(x, g):
    ms = jnp.mean(x * x, axis=-1, keepdims=True)
    return x * lax.rsqrt(ms + EPS) * g


def _norm_matmul_kernel(x_ref, g_ref, w_ref, o_ref, hn_ref, *, scale):
    @pl.when(pl.program_id(1) == 0)
    def _():
        hn_ref[...] = _rmsnorm_rows(x_ref[...], g_ref[...]).astype(BF16)

    acc = jnp.dot(hn_ref[...], w_ref[...], preferred_element_type=F32)
    if scale != 1.0:
        acc = acc * scale
    o_ref[...] = acc.astype(o_ref.dtype)


def norm_matmul(x, gain, w, *, col_start, n_cols, out_dtype, scale=1.0, tm=1024, tn=1024):
    t, d = x.shape
    tm = min(tm, t)
    tn = min(tn, n_cols)
    assert t % tm == 0 and n_cols % tn == 0 and col_start % tn == 0
    off = col_start // tn
    return pl.pallas_call(
        functools.partial(_norm_matmul_kernel, scale=scale),
        grid=(t // tm, n_cols // tn),
        in_specs=[
            pl.BlockSpec((tm, d), lambda i, j: (i, 0)),
            pl.BlockSpec((1, d), lambda i, j: (0, 0)),
            pl.BlockSpec((d, tn), lambda i, j: (0, j + off)),
        ],
        out_specs=pl.BlockSpec((tm, tn), lambda i, j: (i, j)),
        out_shape=jax.ShapeDtypeStruct((t, n_cols), out_dtype),
        scratch_shapes=[pltpu.VMEM((tm, d), BF16)],
        compiler_params=_params("parallel", "arbitrary"),
        name="norm_matmul",
    )(x, gain.reshape(1, d), w)


def _matmul_residual_kernel(y_ref, w_ref, x_ref, o_ref):
    o_ref[...] = x_ref[...] + jnp.dot(y_ref[...], w_ref[...], preferred_element_type=F32)


def matmul_residual(y, w, x, *, tm=1024, tn=512):
    t, k = y.shape
    n = w.shape[1]
    tm = min(tm, t)
    tn = min(tn, n)
    assert t % tm == 0 and n % tn == 0
    return pl.pallas_call(
        _matmul_residual_kernel,
        grid=(t // tm, n // tn),
        in_specs=[
            pl.BlockSpec((tm, k), lambda i, j: (i, 0)),
            pl.BlockSpec((k, tn), lambda i, j: (0, j)),
            pl.BlockSpec((tm, tn), lambda i, j: (i, j)),
        ],
        out_specs=pl.BlockSpec((tm, tn), lambda i, j: (i, j)),
        out_shape=jax.ShapeDtypeStruct((t, n), F32),
        compiler_params=_params("parallel", "arbitrary"),
        name="matmul_residual",
    )(y, w, x)


def _retention_tables(seq):
    c = RET_BLOCK
    log_gamma = jnp.log1p(-jnp.exp2(-5.0 - jnp.arange(RET_HEADS, dtype=F32)))
    idx = jnp.arange(c, dtype=F32)
    diff = idx[:, None] - idx[None, :]
    dmask = jnp.where(diff >= 0, jnp.exp(log_gamma[:, None, None] * jnp.maximum(diff, 0.0)), 0.0)
    xi = jnp.exp(log_gamma[:, None] * (idx[None, :] + 1.0))
    zeta = jnp.exp(log_gamma[:, None] * (c - 1.0 - idx[None, :]))
    cdecay = jnp.exp(log_gamma * c)
    xi = jnp.broadcast_to(xi[:, :, None], (RET_HEADS, c, V7X_LANES))
    zeta = jnp.broadcast_to(zeta[:, :, None], (RET_HEADS, c, V7X_LANES))
    cdecay = jnp.broadcast_to(cdecay[:, None, None], (RET_HEADS, 1, RET_V_DIM))
    half = RET_QK_DIM // 2
    inv = 1.0 / (ROPE_BASE ** (jnp.arange(half, dtype=F32) / half))
    ang = jnp.arange(seq, dtype=F32)[:, None] * inv[None, :]
    return dmask, xi, zeta, cdecay, jnp.cos(ang), jnp.sin(ang)


def _retention_kernel(q_ref, k_ref, v_ref, g_ref, cos_ref, sin_ref, dmask_ref, xi_ref, zeta_ref, cd_ref,
                      gn_ref, y_ref, state_ref):
    @pl.when(pl.program_id(2) == 0)
    def _():
        state_ref[...] = jnp.zeros_like(state_ref)

    cos = cos_ref[...]
    sin = sin_ref[...]
    half = RET_QK_DIM // 2

    def rotate(t):
        t1 = t[:, :half]
        t2 = t[:, half:]
        return jnp.concatenate([t1 * cos - t2 * sin, t1 * sin + t2 * cos], axis=-1)

    q = rotate(q_ref[...])
    k = rotate(k_ref[...]) * (RET_QK_DIM ** -0.5)
    qb = q.astype(BF16)
    kb = k.astype(BF16)
    v = v_ref[...]

    scores = lax.dot_general(qb, kb, NT_DIMS, preferred_element_type=F32) * dmask_ref[...]
    intra = jnp.dot(scores.astype(BF16), v, preferred_element_type=F32)
    state = state_ref[...]
    xi = jnp.concatenate([xi_ref[...]] * (RET_V_DIM // V7X_LANES), axis=-1)
    cross = jnp.dot(qb, state.astype(BF16), preferred_element_type=F32) * xi
    o = intra + cross

    zeta = jnp.concatenate([zeta_ref[...]] * (RET_QK_DIM // V7X_LANES), axis=-1)
    kz = (k * zeta).astype(BF16)
    state_ref[...] = state * cd_ref[...] + lax.dot_general(kz, v, TN_DIMS, preferred_element_type=F32)

    mu = jnp.mean(o, axis=-1, keepdims=True)
    d = o - mu
    var = jnp.mean(d * d, axis=-1, keepdims=True)
    y = d * lax.rsqrt(var + EPS) * gn_ref[...]
    g = g_ref[...]
    y_ref[...] = (g * jax.nn.sigmoid(g) * y).astype(y_ref.dtype)


def retention_core(qk, v, g, gn, batch, seq):
    c = RET_BLOCK
    nc = seq // c
    dmask, xi, zeta, cdecay, cos, sin = _retention_tables(seq)
    row = lambda b, h, n: (b * nc + n, h)
    per_head = lambda b, h, n: (h, 0, 0)
    return pl.pallas_call(
        _retention_kernel,
        grid=(batch, RET_HEADS, nc),
        in_specs=[
            pl.BlockSpec((c, RET_QK_DIM), row),
            pl.BlockSpec((c, RET_QK_DIM), lambda b, h, n: (b * nc + n, RET_HEADS + h)),
            pl.BlockSpec((c, RET_V_DIM), row),
            pl.BlockSpec((c, RET_V_DIM), row),
            pl.BlockSpec((c, RET_QK_DIM // 2), lambda b, h, n: (n, 0)),
            pl.BlockSpec((c, RET_QK_DIM // 2), lambda b, h, n: (n, 0)),
            pl.BlockSpec((None, c, c), per_head),
            pl.BlockSpec((None, c, V7X_LANES), per_head),
            pl.BlockSpec((None, c, V7X_LANES), per_head),
            pl.BlockSpec((None, 1, RET_V_DIM), per_head),
            pl.BlockSpec((1, RET_V_DIM), lambda b, h, n: (0, h)),
        ],
        out_specs=pl.BlockSpec((c, RET_V_DIM), row),
        out_shape=jax.ShapeDtypeStruct(v.shape, BF16),
        scratch_shapes=[pltpu.VMEM((RET_QK_DIM, RET_V_DIM), F32)],
        compiler_params=_params("parallel", "parallel", "arbitrary"),
        name="retention_core",
    )(qk, qk, v, g, cos, sin, dmask, xi, zeta, cdecay, gn.reshape(1, -1))


def _attention_bias(rel_bias):
    i = jnp.arange(ATT_QBLOCK)[:, None]
    m = jnp.arange(ATT_WINDOW)[None, :]
    period = ATT_QBLOCK + ATT_WINDOW - 1
    lag = jnp.arange(period)
    lag = jnp.where(lag < ATT_WINDOW, lag, lag - period)
    rel = jnp.clip(LEFT_CHUNKS * REF_CHUNK - lag, -REL_CLIP, REL_CLIP) + REL_CLIP
    first = rel_bias.astype(F32)[:, rel]
    heads = rel_bias.shape[0]
    skew = jnp.tile(first, (1, ATT_QBLOCK + 1))[:, :ATT_QBLOCK * (period - 1)]
    toeplitz = skew.reshape(heads, ATT_QBLOCK, period - 1)[:, :, :ATT_WINDOW]
    ci = i // REF_CHUNK
    cm = m // REF_CHUNK
    band = (cm >= ci) & (cm <= ci + LEFT_CHUNKS)
    return jnp.where(band[None], toeplitz, MASK_VALUE)


def _attention_kernel(q_ref, k0_ref, k1_ref, k2_ref, v0_ref, v1_ref, v2_ref, bias_ref, o_ref):
    qblk = pl.program_id(1)
    n_prev = LEFT_CHUNKS * REF_CHUNK // ATT_QBLOCK
    col = lax.broadcasted_iota(jnp.int32, (1, ATT_WINDOW), 1)
    start_mask = jnp.where(col >= (n_prev - qblk) * ATT_QBLOCK, 0.0, MASK_VALUE)
    k_refs = (k0_ref, k1_ref, k2_ref)
    v_refs = (v0_ref, v1_ref, v2_ref)

    def one_head(h, carry):
        cols = pl.ds(pl.multiple_of(h * ATT_HEAD_DIM, ATT_HEAD_DIM), ATT_HEAD_DIM)
        q = q_ref[:, cols]
        s = jnp.concatenate(
            [lax.dot_general(q, kr[:, cols], NT_DIMS, preferred_element_type=F32) for kr in k_refs], axis=-1)
        s = s + bias_ref[h] + start_mask
        m = jnp.max(s, axis=-1, keepdims=True)
        p = jnp.exp(s - m)
        p = (p / jnp.sum(p, axis=-1, keepdims=True)).astype(BF16)
        o = jnp.dot(p[:, :ATT_QBLOCK], v_refs[0][:, cols], preferred_element_type=F32)
        for w in range(1, len(v_refs)):
            o = o + jnp.dot(p[:, w * ATT_QBLOCK:(w + 1) * ATT_QBLOCK], v_refs[w][:, cols],
                            preferred_element_type=F32)
        o_ref[:, cols] = o.astype(o_ref.dtype)
        return carry

    lax.fori_loop(0, ATT_HEADS, one_head, 0, unroll=ATT_HEAD_UNROLL)


def attention_core(q, kv, bias, batch, seq):
    t, d = q.shape
    nq = seq // ATT_QBLOCK
    n_prev = LEFT_CHUNKS * REF_CHUNK // ATT_QBLOCK
    assert ATT_WINDOW == (n_prev + 1) * ATT_QBLOCK

    def window(w, part):
        return pl.BlockSpec((ATT_QBLOCK, d), lambda b, n: (b * nq + jnp.maximum(n - n_prev + w, 0), part))

    return pl.pallas_call(
        _attention_kernel,
        grid=(batch, nq),
        in_specs=[pl.BlockSpec((ATT_QBLOCK, d), lambda b, n: (b * nq + n, 0))]
        + [window(w, 0) for w in range(n_prev + 1)]
        + [window(w, 1) for w in range(n_prev + 1)]
        + [pl.BlockSpec(bias.shape, lambda b, n: (0, 0, 0))],
        out_specs=pl.BlockSpec((ATT_QBLOCK, d), lambda b, n: (b * nq + n, 0)),
        out_shape=jax.ShapeDtypeStruct((t, d), BF16),
        compiler_params=_params("parallel", "arbitrary"),
        name="attention_core",
    )(q, kv, kv, kv, kv, kv, kv, bias)


PEER_RANKS = PEER_TOPK + 1
V7X_SUBLANES = 8


def _sorting_network(n):
    pairs = []
    p = 1
    while p < n:
        k = p
        while k >= 1:
            for j in range(k % p, n - k, 2 * k):
                for i in range(min(k, n - j - k)):
                    if (i + j) // (2 * p) == (i + j + k) // (2 * p):
                        pairs.append((i + j, i + j + k))
            k //= 2
        p *= 2
    return pairs


def _sort_descending(vals):
    vals = list(vals)
    for i, j in _sorting_network(len(vals)):
        vals[i], vals[j] = jnp.maximum(vals[i], vals[j]), jnp.minimum(vals[i], vals[j])
    return vals


def _top_ranked(s):
    n = PEER_TOPK
    tiles = [s[r * V7X_SUBLANES:(r + 1) * V7X_SUBLANES, :] for r in range(s.shape[0] // V7X_SUBLANES)]
    assert len(tiles) == n
    top = _sort_descending(tiles)
    shift = V7X_SUBLANES // 2
    while shift >= 1:
        other = [pltpu.roll(v, shift, 0) for v in top]
        top = [jnp.maximum(top[i], other[n - 1 - i]) for i in range(n)]
        k = n // 2
        while k >= 1:
            for i in range(n):
                if i & k == 0:
                    top[i], top[i + k] = jnp.maximum(top[i], top[i + k]), jnp.minimum(top[i], top[i + k])
            k //= 2
        shift //= 2
    below = [jnp.where(t < top[n - 1], t, NEG_INF) for t in tiles]
    while len(below) > 1:
        below = [jnp.maximum(below[2 * i], below[2 * i + 1]) for i in range(len(below) // 2)]
    nxt = below[0]
    shift = V7X_SUBLANES // 2
    while shift >= 1:
        nxt = jnp.maximum(nxt, pltpu.roll(nxt, shift, 0))
        shift //= 2
    return top + [nxt]


def _bf16_twice(x):
    hi = pltpu.bitcast(x.astype(BF16).astype(F32), jnp.uint32)
    return hi | (hi >> 16)


def _peer_prep_kernel(x_ref, g_ref, wqt_ref, sk_ref, hnt_ref, b2_ref, e2_ref, n1_ref, c_ref, qt_ref):
    tm = x_ref.shape[0]
    lane_groups = tm // V7X_LANES
    assert lane_groups <= V7X_SUBLANES
    hnt = _rmsnorm_rows(x_ref[...], g_ref[...]).T.astype(BF16)
    hnt_ref[...] = hnt
    qt_ref[...] = jnp.dot(wqt_ref[...], hnt, preferred_element_type=F32)
    sublane = lax.broadcasted_iota(jnp.int32, (V7X_SUBLANES, V7X_LANES), 0)

    def one_head(h, carry):
        scores = []
        ranked = []
        for part in range(2):
            rows = pl.ds(pl.multiple_of(h * (2 * PEER_HALF) + part * PEER_HALF, PEER_HALF), PEER_HALF)
            s = jnp.dot(sk_ref[part], qt_ref[rows, :].astype(BF16), preferred_element_type=F32)
            scores.append(s)
            per_group = [_top_ranked(s[:, g * V7X_LANES:(g + 1) * V7X_LANES]) for g in range(lane_groups)]
            dense = []
            for r in range(PEER_RANKS):
                tile = per_group[lane_groups - 1][r]
                for g in range(lane_groups - 2, -1, -1):
                    tile = jnp.where(sublane == g, per_group[g][r], tile)
                dense.append(tile)
            ranked.append(dense)

        xs, ys = ranked
        sums = [xs[i] + ys[j] for i in range(PEER_RANKS) for j in range(PEER_RANKS) if (i + 1) * (j + 1) <= PEER_RANKS]
        size = 1
        while size < len(sums):
            size *= 2
        sums = _sort_descending(sums + [jnp.full_like(sums[0], NEG_INF)] * (size - len(sums)))
        z = jnp.ones_like(sums[0])
        for v in sums[1:PEER_TOPK]:
            z = z + jnp.exp(v - sums[0])
        tau = 0.5 * (sums[PEER_TOPK - 1] + sums[PEER_TOPK])
        half_over_z = 0.5 / z

        s1, s2 = scores
        for g in range(lane_groups):
            lanes = slice(g * V7X_LANES, (g + 1) * V7X_LANES)
            row = lambda t, g=g: t[g:g + 1, :]
            s1_g = s1[:, lanes]
            s2_g = s2[:, lanes]
            thr = row(tau) - s1_g
            b2 = jnp.zeros_like(s2_g)
            n1 = jnp.zeros_like(s1_g)
            for r in range(PEER_TOPK):
                y = row(ys[r])
                b2 = b2 + jnp.where(y > s2_g, 1.0, 0.0)
                n1 = n1 + jnp.where(y >= thr, 1.0, 0.0)
            c = jnp.exp(s1_g - row(xs[0])) * row(half_over_z)
            n1_ref[h, :, lanes] = _bf16_twice(n1)
            c_ref[h, :, lanes] = _bf16_twice(c)
            b2_ref[h, g] = pltpu.bitcast(b2.astype(BF16), jnp.uint32)
            e2_ref[h, g] = pltpu.bitcast(jnp.exp(s2_g - row(ys[0])).astype(BF16), jnp.uint32)
        return carry

    lax.fori_loop(0, PEER_HEADS, one_head, 0)


def peer_prep(x, gain, wq_t, sub_keys, *, tm=512):
    t, d = x.shape
    tm = min(tm, t)
    assert t % tm == 0 and tm % V7X_LANES == 0
    nq = wq_t.shape[0]
    fac = jax.ShapeDtypeStruct((PEER_HEADS, PEER_NKEYS, t), jnp.uint32)
    fac_spec = pl.BlockSpec((PEER_HEADS, PEER_NKEYS, tm), lambda i: (0, 0, i))
    blocked = jax.ShapeDtypeStruct((PEER_HEADS, t // V7X_LANES, PEER_NKEYS // 2, V7X_LANES), jnp.uint32)
    blocked_spec = pl.BlockSpec((PEER_HEADS, tm // V7X_LANES, PEER_NKEYS // 2, V7X_LANES), lambda i: (0, i, 0, 0))
    return pl.pallas_call(
        _peer_prep_kernel,
        grid=(t // tm,),
        in_specs=[
            pl.BlockSpec((tm, d), lambda i: (i, 0)),
            pl.BlockSpec((1, d), lambda i: (0, 0)),
            pl.BlockSpec((nq, d), lambda i: (0, 0)),
            pl.BlockSpec(sub_keys.shape, lambda i: (0, 0, 0)),
        ],
        out_specs=[pl.BlockSpec((d, tm), lambda i: (0, i)), blocked_spec, blocked_spec, fac_spec, fac_spec],
        out_shape=[jax.ShapeDtypeStruct((d, t), BF16), blocked, blocked, fac, fac],
        scratch_shapes=[pltpu.VMEM((nq, tm), F32)],
        compiler_params=_params("parallel"),
        name="peer_prep",
    )(x, gain.reshape(1, d), wq_t, sub_keys)


PEER_KEYS_PER_STEP = 8
PEER_ETILE = PEER_KEYS_PER_STEP * PEER_NKEYS
PEER_SUBROWS = 16


def _peer_main_kernel(hnt_ref, u_ref, vt_ref, b2_ref, e2_ref, n1_ref, c_ref, x_ref, o_ref,
                      acc_ref, at_ref, p_ref):
    j = pl.program_id(1)
    tm = hnt_ref.shape[1]
    lane_groups = tm // V7X_LANES

    @pl.when(j == 0)
    def _():
        acc_ref[...] = jnp.zeros_like(acc_ref)

    halves = 2
    keys_per_half = PEER_KEYS_PER_STEP // halves
    rows_per_half = PEER_ETILE // halves
    words = PEER_SUBROWS // 2
    word_tile = (words, V7X_LANES)

    def packed(x):
        return pltpu.bitcast(x, BF16)

    def pre_activation(e):
        rows = slice(e * rows_per_half, (e + 1) * rows_per_half)
        pre_all = jnp.dot(u_ref[rows, :], hnt_ref[...], preferred_element_type=F32)
        for lg in range(lane_groups):
            at_ref[lg, rows, :] = pre_all[:, lg * V7X_LANES:(lg + 1) * V7X_LANES]

    def weighted_gelu(e):
        for ka in range(e * keys_per_half, (e + 1) * keys_per_half):
            for lg in range(lane_groups):
                lanes = slice(lg * V7X_LANES, (lg + 1) * V7X_LANES)
                n1 = [packed(jnp.broadcast_to(n1_ref[h, ka:ka + 1, lanes], word_tile)) for h in range(PEER_HEADS)]
                cw = [packed(jnp.broadcast_to(c_ref[h, ka:ka + 1, lanes], word_tile)) for h in range(PEER_HEADS)]
                for bg in range(PEER_NKEYS // PEER_SUBROWS):
                    wrows = slice(bg * words, (bg + 1) * words)
                    w = None
                    for h in range(PEER_HEADS):
                        term = jnp.where(packed(b2_ref[h, lg, wrows, :]) < n1[h],
                                         packed(e2_ref[h, lg, wrows, :]) * cw[h], 0.0)
                        w = term if w is None else w + term
                    first = ka * PEER_NKEYS + bg * PEER_SUBROWS
                    pre = at_ref[lg, first:first + PEER_SUBROWS, :]
                    act = pre * (1.0 + lax.erf(pre * (2.0 ** -0.5)))
                    p_ref[lg, first // 2:first // 2 + words, :] = pltpu.bitcast(act.astype(BF16) * w, jnp.uint32)

    def value_matmul(e):
        rows = slice(e * rows_per_half, (e + 1) * rows_per_half)
        wrows = slice(e * rows_per_half // 2, (e + 1) * rows_per_half // 2)
        p_half = jnp.concatenate([packed(p_ref[lg, wrows, :]) for lg in range(lane_groups)], axis=-1)
        acc_ref[...] += jnp.dot(vt_ref[:, rows], p_half, preferred_element_type=F32)

    pre_activation(0)
    pre_activation(1)
    weighted_gelu(0)
    value_matmul(0)
    weighted_gelu(1)
    value_matmul(1)

    @pl.when(j == pl.num_programs(1) - 1)
    def _():
        o_ref[...] = x_ref[...] + acc_ref[...].T


def peer_main(hn_t, u, v_t, b2, e2, n1, c, x, *, tm=512):
    t, d = x.shape
    n_exp = u.shape[0]
    tm = min(tm, t)
    assert t % tm == 0 and tm % V7X_LANES == 0
    assert n_exp % PEER_ETILE == 0 and n_exp == PEER_NKEYS * PEER_NKEYS
    lane_groups = tm // V7X_LANES
    all_keys = pl.BlockSpec((PEER_HEADS, lane_groups, PEER_NKEYS // 2, V7X_LANES), lambda i, j: (0, i, 0, 0))
    step_keys = pl.BlockSpec((PEER_HEADS, PEER_KEYS_PER_STEP, tm), lambda i, j: (0, j, i))
    return pl.pallas_call(
        _peer_main_kernel,
        grid=(t // tm, n_exp // PEER_ETILE),
        in_specs=[
            pl.BlockSpec((d, tm), lambda i, j: (0, i)),
            pl.BlockSpec((PEER_ETILE, d), lambda i, j: (j, 0)),
            pl.BlockSpec((d, PEER_ETILE), lambda i, j: (0, j)),
            all_keys, all_keys, step_keys, step_keys,
            pl.BlockSpec((tm, d), lambda i, j: (i, 0)),
        ],
        out_specs=pl.BlockSpec((tm, d), lambda i, j: (i, 0)),
        out_shape=jax.ShapeDtypeStruct((t, d), F32),
        scratch_shapes=[
            pltpu.VMEM((d, tm), F32),
            pltpu.VMEM((lane_groups, PEER_ETILE, V7X_LANES), F32),
            pltpu.VMEM((lane_groups, PEER_ETILE // 2, V7X_LANES), jnp.uint32),
        ],
        compiler_params=_params("parallel", "arbitrary"),
        name="peer_main",
    )(hn_t, u, v_t, b2, e2, n1, c, x)


def peer_layer(x, gain, wq_t, sub_keys, u, v_t):
    hn_t, b2, e2, n1, c = peer_prep(x, gain, wq_t, sub_keys)
    return peer_main(hn_t, u, v_t, b2, e2, n1, c, x)


def _final_norm_kernel(x_ref, g_ref, o_ref):
    o_ref[...] = _rmsnorm_rows(x_ref[...], g_ref[...])


def final_norm(x, gain, *, tm=1024):
    t, d = x.shape
    tm = min(tm, t)
    return pl.pallas_call(
        _final_norm_kernel,
        grid=(t // tm,),
        in_specs=[pl.BlockSpec((tm, d), lambda i: (i, 0)), pl.BlockSpec((1, d), lambda i: (0, 0))],
        out_specs=pl.BlockSpec((tm, d), lambda i: (i, 0)),
        out_shape=jax.ShapeDtypeStruct((t, d), F32),
        compiler_params=_params("parallel"),
        name="final_norm",
    )(x, gain.reshape(1, d))


def kernel(x, ln_mix, ln_ffn, ret_w_in, ret_w_out, ret_gn, kv_norm, w_kv, att_w_q, att_w_o, att_rel_bias,
           peer_w_q, peer_sub_keys, peer_u, peer_v, ln_final):
    batch, seq, d = x.shape
    depth = ln_mix.shape[0]
    n_a = ret_w_in.shape[0]
    xt = x.reshape(batch * seq, d)
    qk_width = 2 * RET_HEADS * RET_QK_DIM
    v_width = RET_HEADS * RET_V_DIM
    kv = None
    for l in range(depth):
        if l < n_a:
            w_in = ret_w_in[l].astype(BF16)
            qk = norm_matmul(xt, ln_mix[l], w_in, col_start=0, n_cols=qk_width, out_dtype=F32)
            v = norm_matmul(xt, ln_mix[l], w_in, col_start=qk_width, n_cols=v_width, out_dtype=BF16)
            g = norm_matmul(xt, ln_mix[l], w_in, col_start=qk_width + v_width, n_cols=v_width, out_dtype=F32)
            y = retention_core(qk, v, g, ret_gn[l], batch, seq)
            xt = matmul_residual(y, ret_w_out[l].astype(BF16), xt)
        else:
            j = l - n_a
            q = norm_matmul(xt, ln_mix[l], att_w_q[j].astype(BF16), col_start=0, n_cols=d, out_dtype=BF16,
                            scale=ATT_HEAD_DIM ** -0.5)
            a = attention_core(q, kv, _attention_bias(att_rel_bias[j]), batch, seq)
            xt = matmul_residual(a, att_w_o[j].astype(BF16), xt)
        xt = peer_layer(xt, ln_ffn[l], peer_w_q[l].T.astype(BF16), peer_sub_keys[l].astype(BF16),
                        peer_u[l].astype(BF16), peer_v[l].T.astype(BF16))
        if l == n_a - 1:
            kv = norm_matmul(xt, kv_norm, w_kv.astype(BF16), col_start=0, n_cols=2 * d, out_dtype=BF16)
    return final_norm(xt, ln_final).reshape(batch, seq, d)
```

```python
import functools

import jax
import jax.numpy as jnp
from jax import lax
from jax.experimental import pallas as pl
from jax.experimental.pallas import tpu as pltpu

F32 = jnp.float32
BF16 = jnp.bfloat16

EPS = 1e-6
ROPE_BASE = 10000.0
REF_CHUNK = 64
RET_HEADS = 8
RET_QK_DIM = 256
RET_V_DIM = 512
RET_BLOCK = 512
ATT_HEADS = 16
ATT_HEAD_DIM = 128
LEFT_CHUNKS = 8
REL_CLIP = 128
ATT_QBLOCK = 256
ATT_WINDOW = ATT_QBLOCK + LEFT_CHUNKS * REF_CHUNK
ATT_HEAD_UNROLL = 4
PEER_HEADS = 8
PEER_NKEYS = 128
PEER_HALF = 128
PEER_TOPK = 16
MASK_VALUE = -1e30
NEG_INF = float("-inf")

V7X_LANES = 128
V7X_VMEM_LIMIT_BYTES = 60 * 1024 * 1024

NT_DIMS = (((1,), (1,)), ((), ()))
TN_DIMS = (((0,), (0,)), ((), ()))


def _params(*sem, flags=None):
    return pltpu.CompilerParams(dimension_semantics=sem, vmem_limit_bytes=V7X_VMEM_LIMIT_BYTES, flags=flags)


def _rmsnorm_rows(x, g):
    ms = jnp.mean(x * x, axis=-1, keepdims=True)
    return x * lax.rsqrt(ms + EPS) * g


def _norm_matmul_kernel(x_ref, g_ref, w_ref, o_ref, hn_ref, *, scale):
    @pl.when(pl.program_id(1) == 0)
    def _():
        hn_ref[...] = _rmsnorm_rows(x_ref[...], g_ref[...]).astype(BF16)

    acc = jnp.dot(hn_ref[...], w_ref[...], preferred_element_type=F32)
    if scale != 1.0:
        acc = acc * scale
    o_ref[...] = acc.astype(o_ref.dtype)


def norm_matmul(x, gain, w, *, col_start, n_cols, out_dtype, scale=1.0, tm=1024, tn=1024):
    t, d = x.shape
    tm = min(tm, t)
    tn = min(tn, n_cols)
    assert t % tm == 0 and n_cols % tn == 0 and col_start % tn == 0
    off = col_start // tn
    return pl.pallas_call(
        functools.partial(_norm_matmul_kernel, scale=scale),
        grid=(t // tm, n_cols // tn),
        in_specs=[
            pl.BlockSpec((tm, d), lambda i, j: (i, 0)),
            pl.BlockSpec((1, d), lambda i, j: (0, 0)),
            pl.BlockSpec((d, tn), lambda i, j: (0, j + off)),
        ],
        out_specs=pl.BlockSpec((tm, tn), lambda i, j: (i, j)),
        out_shape=jax.ShapeDtypeStruct((t, n_cols), out_dtype),
        scratch_shapes=[pltpu.VMEM((tm, d), BF16)],
        compiler_params=_params("parallel", "arbitrary"),
        name="norm_matmul",
    )(x, gain.reshape(1, d), w)


def _matmul_residual_kernel(y_ref, w_ref, x_ref, o_ref):
    o_ref[...] = x_ref[...] + jnp.dot(y_ref[...], w_ref[...], preferred_element_type=F32)


def matmul_residual(y, w, x, *, tm=1024, tn=512):
    t, k = y.shape
    n = w.shape[1]
    tm = min(tm, t)
    tn = min(tn, n)
    assert t % tm == 0 and n % tn == 0
    return pl.pallas_call(
        _matmul_residual_kernel,
        grid=(t // tm, n // tn),
        in_specs=[
            pl.BlockSpec((tm, k), lambda i, j: (i, 0)),
            pl.BlockSpec((k, tn), lambda i, j: (0, j)),
            pl.BlockSpec((tm, tn), lambda i, j: (i, j)),
        ],
        out_specs=pl.BlockSpec((tm, tn), lambda i, j: (i, j)),
        out_shape=jax.ShapeDtypeStruct((t, n), F32),
        compiler_params=_params("parallel", "arbitrary"),
        name="matmul_residual",
    )(y, w, x)


def _retention_tables(seq):
    c = RET_BLOCK
    log_gamma = jnp.log1p(-jnp.exp2(-5.0 - jnp.arange(RET_HEADS, dtype=F32)))
    idx = jnp.arange(c, dtype=F32)
    diff = idx[:, None] - idx[None, :]
    dmask = jnp.where(diff >= 0, jnp.exp(log_gamma[:, None, None] * jnp.maximum(diff, 0.0)), 0.0)
    xi = jnp.exp(log_gamma[:, None] * (idx[None, :] + 1.0))
    zeta = jnp.exp(log_gamma[:, None] * (c - 1.0 - idx[None, :]))
    cdecay = jnp.exp(log_gamma * c)
    xi = jnp.broadcast_to(xi[:, :, None], (RET_HEADS, c, V7X_LANES))
    zeta = jnp.broadcast_to(zeta[:, :, None], (RET_HEADS, c, V7X_LANES))
    cdecay = jnp.broadcast_to(cdecay[:, None, None], (RET_HEADS, 1, RET_V_DIM))
    half = RET_QK_DIM // 2
    inv = 1.0 / (ROPE_BASE ** (jnp.arange(half, dtype=F32) / half))
    ang = jnp.arange(seq, dtype=F32)[:, None] * inv[None, :]
    return dmask, xi, zeta, cdecay, jnp.cos(ang), jnp.sin(ang)


def _retention_kernel(q_ref, k_ref, v_ref, g_ref, cos_ref, sin_ref, dmask_ref, xi_ref, zeta_ref, cd_ref,
                      gn_ref, y_ref, state_ref):
    @pl.when(pl.program_id(2) == 0)
    def _():
        state_ref[...] = jnp.zeros_like(state_ref)

    cos = cos_ref[...]
    sin = sin_ref[...]
    half = RET_QK_DIM // 2

    def rotate(t):
        t1 = t[:, :half]
        t2 = t[:, half:]
        return jnp.concatenate([t1 * cos - t2 * sin, t1 * sin + t2 * cos], axis=-1)

    q = rotate(q_ref[...])
    k = rotate(k_ref[...]) * (RET_QK_DIM ** -0.5)
    qb = q.astype(BF16)
    kb = k.astype(BF16)
    v = v_ref[...]

    scores = lax.dot_general(qb, kb, NT_DIMS, preferred_element_type=F32) * dmask_ref[...]
    intra = jnp.dot(scores.astype(BF16), v, preferred_element_type=F32)
    state = state_ref[...]
    xi = jnp.concatenate([xi_ref[...]] * (RET_V_DIM // V7X_LANES), axis=-1)
    cross = jnp.dot(qb, state.astype(BF16), preferred_element_type=F32) * xi
    o = intra + cross

    zeta = jnp.concatenate([zeta_ref[...]] * (RET_QK_DIM // V7X_LANES), axis=-1)
    kz = (k * zeta).astype(BF16)
    state_ref[...] = state * cd_ref[...] + lax.dot_general(kz, v, TN_DIMS, preferred_element_type=F32)

    mu = jnp.mean(o, axis=-1, keepdims=True)
    d = o - mu
    var = jnp.mean(d * d, axis=-1, keepdims=True)
    y = d * lax.rsqrt(var + EPS) * gn_ref[...]
    g = g_ref[...]
    y_ref[...] = (g * jax.nn.sigmoid(g) * y).astype(y_ref.dtype)


def retention_core(qk, v, g, gn, batch, seq):
    c = RET_BLOCK
    nc = seq // c
    dmask, xi, zeta, cdecay, cos, sin = _retention_tables(seq)
    row = lambda b, h, n: (b * nc + n, h)
    per_head = lambda b, h, n: (h, 0, 0)
    return pl.pallas_call(
        _retention_kernel,
        grid=(batch, RET_HEADS, nc),
        in_specs=[
            pl.BlockSpec((c, RET_QK_DIM), row),
            pl.BlockSpec((c, RET_QK_DIM), lambda b, h, n: (b * nc + n, RET_HEADS + h)),
            pl.BlockSpec((c, RET_V_DIM), row),
            pl.BlockSpec((c, RET_V_DIM), row),
            pl.BlockSpec((c, RET_QK_DIM // 2), lambda b, h, n: (n, 0)),
            pl.BlockSpec((c, RET_QK_DIM // 2), lambda b, h, n: (n, 0)),
            pl.BlockSpec((None, c, c), per_head),
            pl.BlockSpec((None, c, V7X_LANES), per_head),
            pl.BlockSpec((None, c, V7X_LANES), per_head),
            pl.BlockSpec((None, 1, RET_V_DIM), per_head),
            pl.BlockSpec((1, RET_V_DIM), lambda b, h, n: (0, h)),
        ],
        out_specs=pl.BlockSpec((c, RET_V_DIM), row),
        out_shape=jax.ShapeDtypeStruct(v.shape, BF16),
        scratch_shapes=[pltpu.VMEM((RET_QK_DIM, RET_V_DIM), F32)],
        compiler_params=_params("parallel", "parallel", "arbitrary"),
        name="retention_core",
    )(qk, qk, v, g, cos, sin, dmask, xi, zeta, cdecay, gn.reshape(1, -1))


def _attention_bias(rel_bias):
    i = jnp.arange(ATT_QBLOCK)[:, None]
    m = jnp.arange(ATT_WINDOW)[None, :]
    period = ATT_QBLOCK + ATT_WINDOW - 1
    lag = jnp.arange(period)
    lag = jnp.where(lag < ATT_WINDOW, lag, lag - period)
    rel = jnp.clip(LEFT_CHUNKS * REF_CHUNK - lag, -REL_CLIP, REL_CLIP) + REL_CLIP
    first = rel_bias.astype(F32)[:, rel]
    heads = rel_bias.shape[0]
    skew = jnp.tile(first, (1, ATT_QBLOCK + 1))[:, :ATT_QBLOCK * (period - 1)]
    toeplitz = skew.reshape(heads, ATT_QBLOCK, period - 1)[:, :, :ATT_WINDOW]
    ci = i // REF_CHUNK
    cm = m // REF_CHUNK
    band = (cm >= ci) & (cm <= ci + LEFT_CHUNKS)
    return jnp.where(band[None], toeplitz, MASK_VALUE)


def _attention_kernel(q_ref, k0_ref, k1_ref, k2_ref, v0_ref, v1_ref, v2_ref, bias_ref, o_ref):
    qblk = pl.program_id(1)
    n_prev = LEFT_CHUNKS * REF_CHUNK // ATT_QBLOCK
    col = lax.broadcasted_iota(jnp.int32, (1, ATT_WINDOW), 1)
    start_mask = jnp.where(col >= (n_prev - qblk) * ATT_QBLOCK, 0.0, MASK_VALUE)
    k_refs = (k0_ref, k1_ref, k2_ref)
    v_refs = (v0_ref, v1_ref, v2_ref)

    def one_head(h, carry):
        cols = pl.ds(pl.multiple_of(h * ATT_HEAD_DIM, ATT_HEAD_DIM), ATT_HEAD_DIM)
        q = q_ref[:, cols]
        s = jnp.concatenate(
            [lax.dot_general(q, kr[:, cols], NT_DIMS, preferred_element_type=F32) for kr in k_refs], axis=-1)
        s = s + bias_ref[h] + start_mask
        m = jnp.max(s, axis=-1, keepdims=True)
        p = jnp.exp(s - m)
        p = (p / jnp.sum(p, axis=-1, keepdims=True)).astype(BF16)
        o = jnp.dot(p[:, :ATT_QBLOCK], v_refs[0][:, cols], preferred_element_type=F32)
        for w in range(1, len(v_refs)):
            o = o + jnp.dot(p[:, w * ATT_QBLOCK:(w + 1) * ATT_QBLOCK], v_refs[w][:, cols],
                            preferred_element_type=F32)
        o_ref[:, cols] = o.astype(o_ref.dtype)
        return carry

    lax.fori_loop(0, ATT_HEADS, one_head, 0, unroll=ATT_HEAD_UNROLL)


def attention_core(q, kv, bias, batch, seq):
    t, d = q.shape
    nq = seq // ATT_QBLOCK
    n_prev = LEFT_CHUNKS * REF_CHUNK // ATT_QBLOCK
    assert ATT_WINDOW == (n_prev + 1) * ATT_QBLOCK

    def window(w, part):
        return pl.BlockSpec((ATT_QBLOCK, d), lambda b, n: (b * nq + jnp.maximum(n - n_prev + w, 0), part))

    return pl.pallas_call(
        _attention_kernel,
        grid=(batch, nq),
        in_specs=[pl.BlockSpec((ATT_QBLOCK, d), lambda b, n: (b * nq + n, 0))]
        + [window(w, 0) for w in range(n_prev + 1)]
        + [window(w, 1) for w in range(n_prev + 1)]
        + [pl.BlockSpec(bias.shape, lambda b, n: (0, 0, 0))],
        out_specs=pl.BlockSpec((ATT_QBLOCK, d), lambda b, n: (b * nq + n, 0)),
        out_shape=jax.ShapeDtypeStruct((t, d), BF16),
        compiler_params=_params("parallel", "arbitrary"),
        name="attention_core",
    )(q, kv, kv, kv, kv, kv, kv, bias)


PEER_RANKS = PEER_TOPK + 1
V7X_SUBLANES = 8


def _sorting_network(n):
    pairs = []
    p = 1
    while p < n:
        k = p
        while k >= 1:
            for j in range(k % p, n - k, 2 * k):
                for i in range(min(k, n - j - k)):
                    if (i + j) // (2 * p) == (i + j + k) // (2 * p):
                        pairs.append((i + j, i + j + k))
            k //= 2
        p *= 2
    return pairs


def _sort_descending(vals):
    vals = list(vals)
    for i, j in _sorting_network(len(vals)):
        vals[i], vals[j] = jnp.maximum(vals[i], vals[j]), jnp.minimum(vals[i], vals[j])
    return vals


def _top_ranked(s):
    n = PEER_TOPK
    tiles = [s[r * V7X_SUBLANES:(r + 1) * V7X_SUBLANES, :] for r in range(s.shape[0] // V7X_SUBLANES)]
    assert len(tiles) == n
    top = _sort_descending(tiles)
    shift = V7X_SUBLANES // 2
    while shift >= 1:
        other = [pltpu.roll(v, shift, 0) for v in top]
        top = [jnp.maximum(top[i], other[n - 1 - i]) for i in range(n)]
        k = n // 2
        while k >= 1:
            for i in range(n):
                if i & k == 0:
                    top[i], top[i + k] = jnp.maximum(top[i], top[i + k]), jnp.minimum(top[i], top[i + k])
            k //= 2
        shift //= 2
    below = [jnp.where(t < top[n - 1], t, NEG_INF) for t in tiles]
    while len(below) > 1:
        below = [jnp.maximum(below[2 * i], below[2 * i + 1]) for i in range(len(below) // 2)]
    nxt = below[0]
    shift = V7X_SUBLANES // 2
    while shift >= 1:
        nxt = jnp.maximum(nxt, pltpu.roll(nxt, shift, 0))
        shift //= 2
    return top + [nxt]


def _bf16_twice(x):
    hi = pltpu.bitcast(x.astype(BF16).astype(F32), jnp.uint32)
    return hi | (hi >> 16)


def _peer_prep_kernel(x_ref, g_ref, wqt_ref, sk_ref, hnt_ref, b2_ref, e2_ref, n1_ref, c_ref, qt_ref):
    tm = x_ref.shape[0]
    lane_groups = tm // V7X_LANES
    assert lane_groups <= V7X_SUBLANES
    hnt = _rmsnorm_rows(x_ref[...], g_ref[...]).T.astype(BF16)
    hnt_ref[...] = hnt
    qt_ref[...] = jnp.dot(wqt_ref[...], hnt, preferred_element_type=F32)
    sublane = lax.broadcasted_iota(jnp.int32, (V7X_SUBLANES, V7X_LANES), 0)

    def one_head(h, carry):
        scores = []
        ranked = []
        for part in range(2):
            rows = pl.ds(pl.multiple_of(h * (2 * PEER_HALF) + part * PEER_HALF, PEER_HALF), PEER_HALF)
            s = jnp.dot(sk_ref[part], qt_ref[rows, :].astype(BF16), preferred_element_type=F32)
            scores.append(s)
            per_group = [_top_ranked(s[:, g * V7X_LANES:(g + 1) * V7X_LANES]) for g in range(lane_groups)]
            dense = []
            for r in range(PEER_RANKS):
                tile = per_group[lane_groups - 1][r]
                for g in range(lane_groups - 2, -1, -1):
                    tile = jnp.where(sublane == g, per_group[g][r], tile)
                dense.append(tile)
            ranked.append(dense)

        xs, ys = ranked
        sums = [xs[i] + ys[j] for i in range(PEER_RANKS) for j in range(PEER_RANKS) if (i + 1) * (j + 1) <= PEER_RANKS]
        size = 1
        while size < len(sums):
            size *= 2
        sums = _sort_descending(sums + [jnp.full_like(sums[0], NEG_INF)] * (size - len(sums)))
        z = jnp.ones_like(sums[0])
        for v in sums[1:PEER_TOPK]:
            z = z + jnp.exp(v - sums[0])
        tau = 0.5 * (sums[PEER_TOPK - 1] + sums[PEER_TOPK])
        half_over_z = 0.5 / z

        s1, s2 = scores
        for g in range(lane_groups):
            lanes = slice(g * V7X_LANES, (g + 1) * V7X_LANES)
            row = lambda t, g=g: t[g:g + 1, :]
            s1_g = s1[:, lanes]
            s2_g = s2[:, lanes]
            thr = row(tau) - s1_g
            b2 = jnp.zeros_like(s2_g)
            n1 = jnp.zeros_like(s1_g)
            for r in range(PEER_TOPK):
                y = row(ys[r])
                b2 = b2 + jnp.where(y > s2_g, 1.0, 0.0)
                n1 = n1 + jnp.where(y >= thr, 1.0, 0.0)
            c = jnp.exp(s1_g - row(xs[0])) * row(half_over_z)
            n1_ref[h, :, lanes] = _bf16_twice(n1)
            c_ref[h, :, lanes] = _bf16_twice(c)
            b2_ref[h, g] = pltpu.bitcast(b2.astype(BF16), jnp.uint32)
            e2_ref[h, g] = pltpu.bitcast(jnp.exp(s2_g - row(ys[0])).astype(BF16), jnp.uint32)
        return carry

    lax.fori_loop(0, PEER_HEADS, one_head, 0)


def peer_prep(x, gain, wq_t, sub_keys, *, tm=512):
    t, d = x.shape
    tm = min(tm, t)
    assert t % tm == 0 and tm % V7X_LANES == 0
    nq = wq_t.shape[0]
    fac = jax.ShapeDtypeStruct((PEER_HEADS, PEER_NKEYS, t), jnp.uint32)
    fac_spec = pl.BlockSpec((PEER_HEADS, PEER_NKEYS, tm), lambda i: (0, 0, i))
    blocked = jax.ShapeDtypeStruct((PEER_HEADS, t // V7X_LANES, PEER_NKEYS // 2, V7X_LANES), jnp.uint32)
    blocked_spec = pl.BlockSpec((PEER_HEADS, tm // V7X_LANES, PEER_NKEYS // 2, V7X_LANES), lambda i: (0, i, 0, 0))
    return pl.pallas_call(
        _peer_prep_kernel,
        grid=(t // tm,),
        in_specs=[
            pl.BlockSpec((tm, d), lambda i: (i, 0)),
            pl.BlockSpec((1, d), lambda i: (0, 0)),
            pl.BlockSpec((nq, d), lambda i: (0, 0)),
            pl.BlockSpec(sub_keys.shape, lambda i: (0, 0, 0)),
        ],
        out_specs=[pl.BlockSpec((d, tm), lambda i: (0, i)), blocked_spec, blocked_spec, fac_spec, fac_spec],
        out_shape=[jax.ShapeDtypeStruct((d, t), BF16), blocked, blocked, fac, fac],
        scratch_shapes=[pltpu.VMEM((nq, tm), F32)],
        compiler_params=_params("parallel"),
        name="peer_prep",
    )(x, gain.reshape(1, d), wq_t, sub_keys)


PEER_KEYS_PER_STEP = 8
PEER_ETILE = PEER_KEYS_PER_STEP * PEER_NKEYS
PEER_SUBROWS = 16


def _peer_main_kernel(hnt_ref, u_first_ref, u_next_ref, vt_ref, b2_ref, e2_ref, n1_ref, c_ref, x_ref, o_ref,
                      acc_ref, at_ref, p_ref):
    j = pl.program_id(1)
    tm = hnt_ref.shape[1]
    lane_groups = tm // V7X_LANES

    halves = 2
    keys_per_half = PEER_KEYS_PER_STEP // halves
    rows_per_half = PEER_ETILE // halves
    words = PEER_SUBROWS // 2
    word_tile = (words, V7X_LANES)

    def packed(x):
        return pltpu.bitcast(x, BF16)

    def pre_activation(u_ref, slot):
        pre_all = jnp.dot(u_ref[...], hnt_ref[...], preferred_element_type=F32)
        for lg in range(lane_groups):
            at_ref[slot, lg] = pre_all[:, lg * V7X_LANES:(lg + 1) * V7X_LANES]

    @pl.when(j == 0)
    def _():
        acc_ref[...] = jnp.zeros_like(acc_ref)
        pre_activation(u_first_ref, 0)

    def weighted_gelu(e, slot):
        for ka in range(e * keys_per_half, (e + 1) * keys_per_half):
            for lg in range(lane_groups):
                lanes = slice(lg * V7X_LANES, (lg + 1) * V7X_LANES)
                n1 = [packed(jnp.broadcast_to(n1_ref[h, ka:ka + 1, lanes], word_tile)) for h in range(PEER_HEADS)]
                cw = [packed(jnp.broadcast_to(c_ref[h, ka:ka + 1, lanes], word_tile)) for h in range(PEER_HEADS)]
                for bg in range(PEER_NKEYS // PEER_SUBROWS):
                    wrows = slice(bg * words, (bg + 1) * words)
                    w = None
                    for h in range(PEER_HEADS):
                        term = jnp.where(packed(b2_ref[h, lg, wrows, :]) < n1[h],
                                         packed(e2_ref[h, lg, wrows, :]) * cw[h], 0.0)
                        w = term if w is None else w + term
                    first = ka * PEER_NKEYS + bg * PEER_SUBROWS
                    pre = at_ref[slot, lg, first:first + PEER_SUBROWS, :]
                    act = pre * (1.0 + lax.erf(pre * (2.0 ** -0.5)))
                    p_ref[lg, first // 2:first // 2 + words, :] = pltpu.bitcast(act.astype(BF16) * w, jnp.uint32)

    def value_matmul(e):
        rows = slice(e * rows_per_half, (e + 1) * rows_per_half)
        wrows = slice(e * rows_per_half // 2, (e + 1) * rows_per_half // 2)
        p_half = jnp.concatenate([packed(p_ref[lg, wrows, :]) for lg in range(lane_groups)], axis=-1)
        acc_ref[...] += jnp.dot(vt_ref[:, rows], p_half, preferred_element_type=F32)

    def step(slot):
        pre_activation(u_next_ref, 1 - slot)
        weighted_gelu(0, slot)
        value_matmul(0)
        weighted_gelu(1, slot)
        value_matmul(1)

    for parity in range(2):
        pl.when(lax.rem(j, 2) == parity)(functools.partial(step, parity))

    @pl.when(j == pl.num_programs(1) - 1)
    def _():
        o_ref[...] = x_ref[...] + acc_ref[...].T


def peer_main(hn_t, u, v_t, b2, e2, n1, c, x, *, tm=512):
    t, d = x.shape
    n_exp = u.shape[0]
    tm = min(tm, t)
    assert t % tm == 0 and tm % V7X_LANES == 0
    assert n_exp % PEER_ETILE == 0 and n_exp == PEER_NKEYS * PEER_NKEYS
    lane_groups = tm // V7X_LANES
    all_keys = pl.BlockSpec((PEER_HEADS, lane_groups, PEER_NKEYS // 2, V7X_LANES), lambda i, j: (0, i, 0, 0))
    step_keys = pl.BlockSpec((PEER_HEADS, PEER_KEYS_PER_STEP, tm), lambda i, j: (0, j, i))
    n_j = n_exp // PEER_ETILE
    return pl.pallas_call(
        _peer_main_kernel,
        grid=(t // tm, n_j),
        in_specs=[
            pl.BlockSpec((d, tm), lambda i, j: (0, i)),
            pl.BlockSpec((PEER_ETILE, d), lambda i, j: (0, 0), pipeline_mode=pl.Buffered(1)),
            pl.BlockSpec((PEER_ETILE, d), lambda i, j: (jnp.minimum(j + 1, n_j - 1), 0)),
            pl.BlockSpec((d, PEER_ETILE), lambda i, j: (0, j)),
            all_keys, all_keys, step_keys, step_keys,
            pl.BlockSpec((tm, d), lambda i, j: (i, 0)),
        ],
        out_specs=pl.BlockSpec((tm, d), lambda i, j: (i, 0)),
        out_shape=jax.ShapeDtypeStruct((t, d), F32),
        scratch_shapes=[
            pltpu.VMEM((d, tm), F32),
            pltpu.VMEM((2, lane_groups, PEER_ETILE, V7X_LANES), F32),
            pltpu.VMEM((lane_groups, PEER_ETILE // 2, V7X_LANES), jnp.uint32),
        ],
        compiler_params=_params("parallel", "arbitrary"),
        name="peer_main",
    )(hn_t, u, u, v_t, b2, e2, n1, c, x)


def peer_layer(x, gain, wq_t, sub_keys, u, v_t):
    hn_t, b2, e2, n1, c = peer_prep(x, gain, wq_t, sub_keys)
    return peer_main(hn_t, u, v_t, b2, e2, n1, c, x)


def _final_norm_kernel(x_ref, g_ref, o_ref):
    o_ref[...] = _rmsnorm_rows(x_ref[...], g_ref[...])


def final_norm(x, gain, *, tm=1024):
    t, d = x.shape
    tm = min(tm, t)
    return pl.pallas_call(
        _final_norm_kernel,
        grid=(t // tm,),
        in_specs=[pl.BlockSpec((tm, d), lambda i: (i, 0)), pl.BlockSpec((1, d), lambda i: (0, 0))],
        out_specs=pl.BlockSpec((tm, d), lambda i: (i, 0)),
        out_shape=jax.ShapeDtypeStruct((t, d), F32),
        compiler_params=_params("parallel"),
        name="final_norm",
    )(x, gain.reshape(1, d))


def kernel(x, ln_mix, ln_ffn, ret_w_in, ret_w_out, ret_gn, kv_norm, w_kv, att_w_q, att_w_o, att_rel_bias,
           peer_w_q, peer_sub_keys, peer_u, peer_v, ln_final):
    batch, seq, d = x.shape
    depth = ln_mix.shape[0]
    n_a = ret_w_in.shape[0]
    xt = x.reshape(batch * seq, d)
    qk_width = 2 * RET_HEADS * RET_QK_DIM
    v_width = RET_HEADS * RET_V_DIM
    kv = None
    for l in range(depth):
        if l < n_a:
            w_in = ret_w_in[l].astype(BF16)
            qk = norm_matmul(xt, ln_mix[l], w_in, col_start=0, n_cols=qk_width, out_dtype=F32)
            v = norm_matmul(xt, ln_mix[l], w_in, col_start=qk_width, n_cols=v_width, out_dtype=BF16)
            g = norm_matmul(xt, ln_mix[l], w_in, col_start=qk_width + v_width, n_cols=v_width, out_dtype=F32)
            y = retention_core(qk, v, g, ret_gn[l], batch, seq)
            xt = matmul_residual(y, ret_w_out[l].astype(BF16), xt)
        else:
            j = l - n_a
            q = norm_matmul(xt, ln_mix[l], att_w_q[j].astype(BF16), col_start=0, n_cols=d, out_dtype=BF16,
                            scale=ATT_HEAD_DIM ** -0.5)
            a = attention_core(q, kv, _attention_bias(att_rel_bias[j]), batch, seq)
            xt = matmul_residual(a, att_w_o[j].astype(BF16), xt)
        xt = peer_layer(xt, ln_ffn[l], peer_w_q[l].T.astype(BF16), peer_sub_keys[l].astype(BF16),
                        peer_u[l].astype(BF16), peer_v[l].T.astype(BF16))
        if l == n_a - 1:
            kv = norm_matmul(xt, kv_norm, w_kv.astype(BF16), col_start=0, n_cols=2 * d, out_dtype=BF16)
    return final_norm(xt, ln_final).reshape(batch, seq, d)
```

```python
import functools

import jax
import jax.numpy as jnp
from jax import lax
from jax.experimental import pallas as pl
from jax.experimental.pallas import tpu as pltpu

F32 = jnp.float32
BF16 = jnp.bfloat16

EPS = 1e-6
ROPE_BASE = 10000.0
REF_CHUNK = 64
RET_HEADS = 8
RET_QK_DIM = 256
RET_V_DIM = 512
RET_BLOCK = 512
ATT_HEADS = 16
ATT_HEAD_DIM = 128
LEFT_CHUNKS = 8
REL_CLIP = 128
ATT_QBLOCK = 256
ATT_WINDOW = ATT_QBLOCK + LEFT_CHUNKS * REF_CHUNK
ATT_HEAD_UNROLL = 4
PEER_HEADS = 8
PEER_NKEYS = 128
PEER_HALF = 128
PEER_TOPK = 16
MASK_VALUE = -1e30
NEG_INF = float("-inf")

V7X_LANES = 128
V7X_VMEM_LIMIT_BYTES = 60 * 1024 * 1024

NT_DIMS = (((1,), (1,)), ((), ()))
TN_DIMS = (((0,), (0,)), ((), ()))


def _params(*sem, flags=None):
    return pltpu.CompilerParams(dimension_semantics=sem, vmem_limit_bytes=V7X_VMEM_LIMIT_BYTES, flags=flags)


def _rmsnorm_rows(x, g):
    ms = jnp.mean(x * x, axis=-1, keepdims=True)
    return x * lax.rsqrt(ms + EPS) * g


def _norm_matmul_kernel(x_ref, g_ref, w_ref, o_ref, hn_ref, *, scale):
    @pl.when(pl.program_id(1) == 0)
    def _():
        hn_ref[...] = _rmsnorm_rows(x_ref[...], g_ref[...]).astype(BF16)

    acc = jnp.dot(hn_ref[...], w_ref[...], preferred_element_type=F32)
    if scale != 1.0:
        acc = acc * scale
    o_ref[...] = acc.astype(o_ref.dtype)


def norm_matmul(x, gain, w, *, col_start, n_cols, out_dtype, scale=1.0, tm=1024, tn=1024):
    t, d = x.shape
    tm = min(tm, t)
    tn = min(tn, n_cols)
    assert t % tm == 0 and n_cols % tn == 0 and col_start % tn == 0
    off = col_start // tn
    return pl.pallas_call(
        functools.partial(_norm_matmul_kernel, scale=scale),
        grid=(t // tm, n_cols // tn),
        in_specs=[
            pl.BlockSpec((tm, d), lambda i, j: (i, 0)),
            pl.BlockSpec((1, d), lambda i, j: (0, 0)),
            pl.BlockSpec((d, tn), lambda i, j: (0, j + off)),
        ],
        out_specs=pl.BlockSpec((tm, tn), lambda i, j: (i, j)),
        out_shape=jax.ShapeDtypeStruct((t, n_cols), out_dtype),
        scratch_shapes=[pltpu.VMEM((tm, d), BF16)],
        compiler_params=_params("parallel", "arbitrary"),
        name="norm_matmul",
    )(x, gain.reshape(1, d), w)


def _matmul_residual_kernel(y_ref, w_ref, x_ref, o_ref):
    o_ref[...] = x_ref[...] + jnp.dot(y_ref[...], w_ref[...], preferred_element_type=F32)


def matmul_residual(y, w, x, *, tm=1024, tn=512):
    t, k = y.shape
    n = w.shape[1]
    tm = min(tm, t)
    tn = min(tn, n)
    assert t % tm == 0 and n % tn == 0
    return pl.pallas_call(
        _matmul_residual_kernel,
        grid=(t // tm, n // tn),
        in_specs=[
            pl.BlockSpec((tm, k), lambda i, j: (i, 0)),
            pl.BlockSpec((k, tn), lambda i, j: (0, j)),
            pl.BlockSpec((tm, tn), lambda i, j: (i, j)),
        ],
        out_specs=pl.BlockSpec((tm, tn), lambda i, j: (i, j)),
        out_shape=jax.ShapeDtypeStruct((t, n), F32),
        compiler_params=_params("parallel", "arbitrary"),
        name="matmul_residual",
    )(y, w, x)


def _retention_tables(seq):
    c = RET_BLOCK
    log_gamma = jnp.log1p(-jnp.exp2(-5.0 - jnp.arange(RET_HEADS, dtype=F32)))
    idx = jnp.arange(c, dtype=F32)
    diff = idx[:, None] - idx[None, :]
    dmask = jnp.where(diff >= 0, jnp.exp(log_gamma[:, None, None] * jnp.maximum(diff, 0.0)), 0.0)
    xi = jnp.exp(log_gamma[:, None] * (idx[None, :] + 1.0))
    zeta = jnp.exp(log_gamma[:, None] * (c - 1.0 - idx[None, :]))
    cdecay = jnp.exp(log_gamma * c)
    xi = jnp.broadcast_to(xi[:, :, None], (RET_HEADS, c, V7X_LANES))
    zeta = jnp.broadcast_to(zeta[:, :, None], (RET_HEADS, c, V7X_LANES))
    cdecay = jnp.broadcast_to(cdecay[:, None, None], (RET_HEADS, 1, RET_V_DIM))
    half = RET_QK_DIM // 2
    inv = 1.0 / (ROPE_BASE ** (jnp.arange(half, dtype=F32) / half))
    ang = jnp.arange(seq, dtype=F32)[:, None] * inv[None, :]
    return dmask, xi, zeta, cdecay, jnp.cos(ang), jnp.sin(ang)


def _retention_kernel(q_ref, k_ref, v_ref, g_ref, cos_ref, sin_ref, dmask_ref, xi_ref, zeta_ref, cd_ref,
                      gn_ref, y_ref, state_ref):
    @pl.when(pl.program_id(2) == 0)
    def _():
        state_ref[...] = jnp.zeros_like(state_ref)

    cos = cos_ref[...]
    sin = sin_ref[...]
    half = RET_QK_DIM // 2

    def rotate(t):
        t1 = t[:, :half]
        t2 = t[:, half:]
        return jnp.concatenate([t1 * cos - t2 * sin, t1 * sin + t2 * cos], axis=-1)

    q = rotate(q_ref[...])
    k = rotate(k_ref[...]) * (RET_QK_DIM ** -0.5)
    qb = q.astype(BF16)
    kb = k.astype(BF16)
    v = v_ref[...]

    scores = lax.dot_general(qb, kb, NT_DIMS, preferred_element_type=F32) * dmask_ref[...]
    intra = jnp.dot(scores.astype(BF16), v, preferred_element_type=F32)
    state = state_ref[...]
    xi = jnp.concatenate([xi_ref[...]] * (RET_V_DIM // V7X_LANES), axis=-1)
    cross = jnp.dot(qb, state.astype(BF16), preferred_element_type=F32) * xi
    o = intra + cross

    zeta = jnp.concatenate([zeta_ref[...]] * (RET_QK_DIM // V7X_LANES), axis=-1)
    kz = (k * zeta).astype(BF16)
    state_ref[...] = state * cd_ref[...] + lax.dot_general(kz, v, TN_DIMS, preferred_element_type=F32)

    mu = jnp.mean(o, axis=-1, keepdims=True)
    d = o - mu
    var = jnp.mean(d * d, axis=-1, keepdims=True)
    y = d * lax.rsqrt(var + EPS) * gn_ref[...]
    g = g_ref[...]
    y_ref[...] = (g * jax.nn.sigmoid(g) * y).astype(y_ref.dtype)


def retention_core(qk, v, g, gn, batch, seq):
    c = RET_BLOCK
    nc = seq // c
    dmask, xi, zeta, cdecay, cos, sin = _retention_tables(seq)
    row = lambda b, h, n: (b * nc + n, h)
    per_head = lambda b, h, n: (h, 0, 0)
    return pl.pallas_call(
        _retention_kernel,
        grid=(batch, RET_HEADS, nc),
        in_specs=[
            pl.BlockSpec((c, RET_QK_DIM), row),
            pl.BlockSpec((c, RET_QK_DIM), lambda b, h, n: (b * nc + n, RET_HEADS + h)),
            pl.BlockSpec((c, RET_V_DIM), row),
            pl.BlockSpec((c, RET_V_DIM), row),
            pl.BlockSpec((c, RET_QK_DIM // 2), lambda b, h, n: (n, 0)),
            pl.BlockSpec((c, RET_QK_DIM // 2), lambda b, h, n: (n, 0)),
            pl.BlockSpec((None, c, c), per_head),
            pl.BlockSpec((None, c, V7X_LANES), per_head),
            pl.BlockSpec((None, c, V7X_LANES), per_head),
            pl.BlockSpec((None, 1, RET_V_DIM), per_head),
            pl.BlockSpec((1, RET_V_DIM), lambda b, h, n: (0, h)),
        ],
        out_specs=pl.BlockSpec((c, RET_V_DIM), row),
        out_shape=jax.ShapeDtypeStruct(v.shape, BF16),
        scratch_shapes=[pltpu.VMEM((RET_QK_DIM, RET_V_DIM), F32)],
        compiler_params=_params("parallel", "parallel", "arbitrary"),
        name="retention_core",
    )(qk, qk, v, g, cos, sin, dmask, xi, zeta, cdecay, gn.reshape(1, -1))


def _attention_bias(rel_bias):
    i = jnp.arange(ATT_QBLOCK)[:, None]
    m = jnp.arange(ATT_WINDOW)[None, :]
    period = ATT_QBLOCK + ATT_WINDOW - 1
    lag = jnp.arange(period)
    lag = jnp.where(lag < ATT_WINDOW, lag, lag - period)
    rel = jnp.clip(LEFT_CHUNKS * REF_CHUNK - lag, -REL_CLIP, REL_CLIP) + REL_CLIP
    first = rel_bias.astype(F32)[:, rel]
    heads = rel_bias.shape[0]
    skew = jnp.tile(first, (1, ATT_QBLOCK + 1))[:, :ATT_QBLOCK * (period - 1)]
    toeplitz = skew.reshape(heads, ATT_QBLOCK, period - 1)[:, :, :ATT_WINDOW]
    ci = i // REF_CHUNK
    cm = m // REF_CHUNK
    band = (cm >= ci) & (cm <= ci + LEFT_CHUNKS)
    return jnp.where(band[None], toeplitz, MASK_VALUE)


def _attention_kernel(q_ref, k0_ref, k1_ref, k2_ref, v0_ref, v1_ref, v2_ref, bias_ref, o_ref):
    qblk = pl.program_id(1)
    n_prev = LEFT_CHUNKS * REF_CHUNK // ATT_QBLOCK
    col = lax.broadcasted_iota(jnp.int32, (1, ATT_WINDOW), 1)
    start_mask = jnp.where(col >= (n_prev - qblk) * ATT_QBLOCK, 0.0, MASK_VALUE)
    k_refs = (k0_ref, k1_ref, k2_ref)
    v_refs = (v0_ref, v1_ref, v2_ref)

    def one_head(h, carry):
        cols = pl.ds(pl.multiple_of(h * ATT_HEAD_DIM, ATT_HEAD_DIM), ATT_HEAD_DIM)
        q = q_ref[:, cols]
        s = jnp.concatenate(
            [lax.dot_general(q, kr[:, cols], NT_DIMS, preferred_element_type=F32) for kr in k_refs], axis=-1)
        s = s + bias_ref[h] + start_mask
        m = jnp.max(s, axis=-1, keepdims=True)
        p = jnp.exp(s - m)
        p = (p / jnp.sum(p, axis=-1, keepdims=True)).astype(BF16)
        o = jnp.dot(p[:, :ATT_QBLOCK], v_refs[0][:, cols], preferred_element_type=F32)
        for w in range(1, len(v_refs)):
            o = o + jnp.dot(p[:, w * ATT_QBLOCK:(w + 1) * ATT_QBLOCK], v_refs[w][:, cols],
                            preferred_element_type=F32)
        o_ref[:, cols] = o.astype(o_ref.dtype)
        return carry

    lax.fori_loop(0, ATT_HEADS, one_head, 0, unroll=ATT_HEAD_UNROLL)


def attention_core(q, kv, bias, batch, seq):
    t, d = q.shape
    nq = seq // ATT_QBLOCK
    n_prev = LEFT_CHUNKS * REF_CHUNK // ATT_QBLOCK
    assert ATT_WINDOW == (n_prev + 1) * ATT_QBLOCK

    def window(w, part):
        return pl.BlockSpec((ATT_QBLOCK, d), lambda b, n: (b * nq + jnp.maximum(n - n_prev + w, 0), part))

    return pl.pallas_call(
        _attention_kernel,
        grid=(batch, nq),
        in_specs=[pl.BlockSpec((ATT_QBLOCK, d), lambda b, n: (b * nq + n, 0))]
        + [window(w, 0) for w in range(n_prev + 1)]
        + [window(w, 1) for w in range(n_prev + 1)]
        + [pl.BlockSpec(bias.shape, lambda b, n: (0, 0, 0))],
        out_specs=pl.BlockSpec((ATT_QBLOCK, d), lambda b, n: (b * nq + n, 0)),
        out_shape=jax.ShapeDtypeStruct((t, d), BF16),
        compiler_params=_params("parallel", "arbitrary"),
        name="attention_core",
    )(q, kv, kv, kv, kv, kv, kv, bias)


PEER_RANKS = PEER_TOPK + 1
V7X_SUBLANES = 8


def _sorting_network(n):
    pairs = []
    p = 1
    while p < n:
        k = p
        while k >= 1:
            for j in range(k % p, n - k, 2 * k):
                for i in range(min(k, n - j - k)):
                    if (i + j) // (2 * p) == (i + j + k) // (2 * p):
                        pairs.append((i + j, i + j + k))
            k //= 2
        p *= 2
    return pairs


def _sort_descending(vals):
    vals = list(vals)
    for i, j in _sorting_network(len(vals)):
        vals[i], vals[j] = jnp.maximum(vals[i], vals[j]), jnp.minimum(vals[i], vals[j])
    return vals


def _top_ranked(s):
    n = PEER_TOPK
    tiles = [s[r * V7X_SUBLANES:(r + 1) * V7X_SUBLANES, :] for r in range(s.shape[0] // V7X_SUBLANES)]
    assert len(tiles) == n
    top = _sort_descending(tiles)
    shift = V7X_SUBLANES // 2
    while shift >= 1:
        other = [pltpu.roll(v, shift, 0) for v in top]
        top = [jnp.maximum(top[i], other[n - 1 - i]) for i in range(n)]
        k = n // 2
        while k >= 1:
            for i in range(n):
                if i & k == 0:
                    top[i], top[i + k] = jnp.maximum(top[i], top[i + k]), jnp.minimum(top[i], top[i + k])
            k //= 2
        shift //= 2
    below = [jnp.where(t < top[n - 1], t, NEG_INF) for t in tiles]
    while len(below) > 1:
        below = [jnp.maximum(below[2 * i], below[2 * i + 1]) for i in range(len(below) // 2)]
    nxt = below[0]
    shift = V7X_SUBLANES // 2
    while shift >= 1:
        nxt = jnp.maximum(nxt, pltpu.roll(nxt, shift, 0))
        shift //= 2
    return top + [nxt]


def _bf16_twice(x):
    hi = pltpu.bitcast(x.astype(BF16).astype(F32), jnp.uint32)
    return hi | (hi >> 16)


def _peer_prep_kernel(x_ref, g_ref, wqt_ref, sk_ref, hnt_ref, b2_ref, e2_ref, n1_ref, c_ref, qt_ref):
    tm = x_ref.shape[0]
    lane_groups = tm // V7X_LANES
    assert lane_groups <= V7X_SUBLANES
    hnt = _rmsnorm_rows(x_ref[...], g_ref[...]).T.astype(BF16)
    hnt_ref[...] = hnt
    qt_ref[...] = jnp.dot(wqt_ref[...], hnt, preferred_element_type=F32)
    sublane = lax.broadcasted_iota(jnp.int32, (V7X_SUBLANES, V7X_LANES), 0)

    def one_head(h, carry):
        scores = []
        ranked = []
        for part in range(2):
            rows = pl.ds(pl.multiple_of(h * (2 * PEER_HALF) + part * PEER_HALF, PEER_HALF), PEER_HALF)
            s = jnp.dot(sk_ref[part], qt_ref[rows, :].astype(BF16), preferred_element_type=F32)
            scores.append(s)
            per_group = [_top_ranked(s[:, g * V7X_LANES:(g + 1) * V7X_LANES]) for g in range(lane_groups)]
            dense = []
            for r in range(PEER_RANKS):
                tile = per_group[lane_groups - 1][r]
                for g in range(lane_groups - 2, -1, -1):
                    tile = jnp.where(sublane == g, per_group[g][r], tile)
                dense.append(tile)
            ranked.append(dense)

        xs, ys = ranked
        sums = [xs[i] + ys[j] for i in range(PEER_RANKS) for j in range(PEER_RANKS) if (i + 1) * (j + 1) <= PEER_RANKS]
        size = 1
        while size < len(sums):
            size *= 2
        sums = _sort_descending(sums + [jnp.full_like(sums[0], NEG_INF)] * (size - len(sums)))
        z = jnp.ones_like(sums[0])
        for v in sums[1:PEER_TOPK]:
            z = z + jnp.exp(v - sums[0])
        tau = 0.5 * (sums[PEER_TOPK - 1] + sums[PEER_TOPK])
        half_over_z = 0.5 / z

        s1, s2 = scores
        for g in range(lane_groups):
            lanes = slice(g * V7X_LANES, (g + 1) * V7X_LANES)
            row = lambda t, g=g: t[g:g + 1, :]
            s1_g = s1[:, lanes]
            s2_g = s2[:, lanes]
            thr = row(tau) - s1_g
            b2 = jnp.zeros_like(s2_g)
            n1 = jnp.zeros_like(s1_g)
            for r in range(PEER_TOPK):
                y = row(ys[r])
                b2 = b2 + jnp.where(y > s2_g, 1.0, 0.0)
                n1 = n1 + jnp.where(y >= thr, 1.0, 0.0)
            c = jnp.exp(s1_g - row(xs[0])) * row(half_over_z)
            n1_ref[h, :, lanes] = _bf16_twice(n1)
            c_ref[h, :, lanes] = _bf16_twice(c)
            b2_ref[h, g] = pltpu.bitcast(b2.astype(BF16), jnp.uint32)
            e2_ref[h, g] = pltpu.bitcast(jnp.exp(s2_g - row(ys[0])).astype(BF16), jnp.uint32)
        return carry

    lax.fori_loop(0, PEER_HEADS, one_head, 0)


def peer_prep(x, gain, wq_t, sub_keys, *, tm=512):
    t, d = x.shape
    tm = min(tm, t)
    assert t % tm == 0 and tm % V7X_LANES == 0
    nq = wq_t.shape[0]
    fac = jax.ShapeDtypeStruct((PEER_HEADS, PEER_NKEYS, t), jnp.uint32)
    fac_spec = pl.BlockSpec((PEER_HEADS, PEER_NKEYS, tm), lambda i: (0, 0, i))
    blocked = jax.ShapeDtypeStruct((PEER_HEADS, t // V7X_LANES, PEER_NKEYS // 2, V7X_LANES), jnp.uint32)
    blocked_spec = pl.BlockSpec((PEER_HEADS, tm // V7X_LANES, PEER_NKEYS // 2, V7X_LANES), lambda i: (0, i, 0, 0))
    return pl.pallas_call(
        _peer_prep_kernel,
        grid=(t // tm,),
        in_specs=[
            pl.BlockSpec((tm, d), lambda i: (i, 0)),
            pl.BlockSpec((1, d), lambda i: (0, 0)),
            pl.BlockSpec((nq, d), lambda i: (0, 0)),
            pl.BlockSpec(sub_keys.shape, lambda i: (0, 0, 0)),
        ],
        out_specs=[pl.BlockSpec((d, tm), lambda i: (0, i)), blocked_spec, blocked_spec, fac_spec, fac_spec],
        out_shape=[jax.ShapeDtypeStruct((d, t), BF16), blocked, blocked, fac, fac],
        scratch_shapes=[pltpu.VMEM((nq, tm), F32)],
        compiler_params=_params("parallel"),
        name="peer_prep",
    )(x, gain.reshape(1, d), wq_t, sub_keys)


PEER_KEYS_PER_STEP = 8
PEER_ETILE = PEER_KEYS_PER_STEP * PEER_NKEYS
PEER_SUBROWS = 16


def _peer_main_kernel(hnt_first_ref, u_first_ref, hnt_ref, u_ref, vt_ref, b2_ref, e2_ref, n1_ref, c_ref, x_ref,
                      o_ref, acc_ref, at_ref, p_ref, *, tiles_per_block):
    g = pl.program_id(0)
    j = lax.rem(g, tiles_per_block)
    tm = hnt_ref.shape[1]
    lane_groups = tm // V7X_LANES

    halves = 2
    keys_per_half = PEER_KEYS_PER_STEP // halves
    rows_per_half = PEER_ETILE // halves
    words = PEER_SUBROWS // 2
    word_tile = (words, V7X_LANES)

    def packed(x):
        return pltpu.bitcast(x, BF16)

    def pre_activation(u_tile_ref, hn_ref, slot):
        pre_all = jnp.dot(u_tile_ref[...], hn_ref[...], preferred_element_type=F32)
        for lg in range(lane_groups):
            at_ref[slot, lg] = pre_all[:, lg * V7X_LANES:(lg + 1) * V7X_LANES]

    @pl.when(g == 0)
    def _():
        pre_activation(u_first_ref, hnt_first_ref, 0)

    @pl.when(j == 0)
    def _():
        acc_ref[...] = jnp.zeros_like(acc_ref)

    def weighted_gelu(e, slot):
        for ka in range(e * keys_per_half, (e + 1) * keys_per_half):
            for lg in range(lane_groups):
                lanes = slice(lg * V7X_LANES, (lg + 1) * V7X_LANES)
                n1 = [packed(jnp.broadcast_to(n1_ref[h, ka:ka + 1, lanes], word_tile)) for h in range(PEER_HEADS)]
                cw = [packed(jnp.broadcast_to(c_ref[h, ka:ka + 1, lanes], word_tile)) for h in range(PEER_HEADS)]
                for bg in range(PEER_NKEYS // PEER_SUBROWS):
                    wrows = slice(bg * words, (bg + 1) * words)
                    w = None
                    for h in range(PEER_HEADS):
                        term = jnp.where(packed(b2_ref[h, lg, wrows, :]) < n1[h],
                                         packed(e2_ref[h, lg, wrows, :]) * cw[h], 0.0)
                        w = term if w is None else w + term
                    first = ka * PEER_NKEYS + bg * PEER_SUBROWS
                    pre = at_ref[slot, lg, first:first + PEER_SUBROWS, :]
                    act = pre * (1.0 + lax.erf(pre * (2.0 ** -0.5)))
                    p_ref[lg, first // 2:first // 2 + words, :] = pltpu.bitcast(act.astype(BF16) * w, jnp.uint32)

    def value_matmul(e):
        rows = slice(e * rows_per_half, (e + 1) * rows_per_half)
        wrows = slice(e * rows_per_half // 2, (e + 1) * rows_per_half // 2)
        p_half = jnp.concatenate([packed(p_ref[lg, wrows, :]) for lg in range(lane_groups)], axis=-1)
        acc_ref[...] += jnp.dot(vt_ref[:, rows], p_half, preferred_element_type=F32)

    def step(slot):
        pre_activation(u_ref, hnt_ref, 1 - slot)
        weighted_gelu(0, slot)
        value_matmul(0)
        weighted_gelu(1, slot)
        value_matmul(1)

    for parity in range(2):
        pl.when(lax.rem(g, 2) == parity)(functools.partial(step, parity))

    @pl.when(j == tiles_per_block - 1)
    def _():
        o_ref[...] = x_ref[...] + acc_ref[...].T


def peer_main(hn_t, u, v_t, b2, e2, n1, c, x, *, tm=512):
    t, d = x.shape
    n_exp = u.shape[0]
    tm = min(tm, t)
    assert t % tm == 0 and tm % V7X_LANES == 0
    assert n_exp % PEER_ETILE == 0 and n_exp == PEER_NKEYS * PEER_NKEYS
    lane_groups = tm // V7X_LANES
    n_i = t // tm
    n_j = n_exp // PEER_ETILE
    assert n_j % 2 == 0

    def block(g):
        return g // n_j

    def tile(g):
        return lax.rem(g, n_j)

    def nxt(g):
        return jnp.minimum(g + 1, n_i * n_j - 1)

    all_keys = pl.BlockSpec((PEER_HEADS, lane_groups, PEER_NKEYS // 2, V7X_LANES), lambda g: (0, block(g), 0, 0))
    step_keys = pl.BlockSpec((PEER_HEADS, PEER_KEYS_PER_STEP, tm), lambda g: (0, tile(g), block(g)))
    return pl.pallas_call(
        functools.partial(_peer_main_kernel, tiles_per_block=n_j),
        grid=(n_i * n_j,),
        in_specs=[
            pl.BlockSpec((d, tm), lambda g: (0, 0), pipeline_mode=pl.Buffered(1)),
            pl.BlockSpec((PEER_ETILE, d), lambda g: (0, 0), pipeline_mode=pl.Buffered(1)),
            pl.BlockSpec((d, tm), lambda g: (0, block(nxt(g)))),
            pl.BlockSpec((PEER_ETILE, d), lambda g: (tile(nxt(g)), 0)),
            pl.BlockSpec((d, PEER_ETILE), lambda g: (0, tile(g))),
            all_keys, all_keys, step_keys, step_keys,
            pl.BlockSpec((tm, d), lambda g: (block(g), 0)),
        ],
        out_specs=pl.BlockSpec((tm, d), lambda g: (block(g), 0)),
        out_shape=jax.ShapeDtypeStruct((t, d), F32),
        scratch_shapes=[
            pltpu.VMEM((d, tm), F32),
            pltpu.VMEM((2, lane_groups, PEER_ETILE, V7X_LANES), F32),
            pltpu.VMEM((lane_groups, PEER_ETILE // 2, V7X_LANES), jnp.uint32),
        ],
        compiler_params=_params("arbitrary"),
        name="peer_main",
    )(hn_t, u, hn_t, u, v_t, b2, e2, n1, c, x)


def peer_layer(x, gain, wq_t, sub_keys, u, v_t):
    hn_t, b2, e2, n1, c = peer_prep(x, gain, wq_t, sub_keys)
    return peer_main(hn_t, u, v_t, b2, e2, n1, c, x)


def _final_norm_kernel(x_ref, g_ref, o_ref):
    o_ref[...] = _rmsnorm_rows(x_ref[...], g_ref[...])


def final_norm(x, gain, *, tm=1024):
    t, d = x.shape
    tm = min(tm, t)
    return pl.pallas_call(
        _final_norm_kernel,
        grid=(t // tm,),
        in_specs=[pl.BlockSpec((tm, d), lambda i: (i, 0)), pl.BlockSpec((1, d), lambda i: (0, 0))],
        out_specs=pl.BlockSpec((tm, d), lambda i: (i, 0)),
        out_shape=jax.ShapeDtypeStruct((t, d), F32),
        compiler_params=_params("parallel"),
        name="final_norm",
    )(x, gain.reshape(1, d))


def kernel(x, ln_mix, ln_ffn, ret_w_in, ret_w_out, ret_gn, kv_norm, w_kv, att_w_q, att_w_o, att_rel_bias,
           peer_w_q, peer_sub_keys, peer_u, peer_v, ln_final):
    batch, seq, d = x.shape
    depth = ln_mix.shape[0]
    n_a = ret_w_in.shape[0]
    xt = x.reshape(batch * seq, d)
    qk_width = 2 * RET_HEADS * RET_QK_DIM
    v_width = RET_HEADS * RET_V_DIM
    kv = None
    for l in range(depth):
        if l < n_a:
            w_in = ret_w_in[l].astype(BF16)
            qk = norm_matmul(xt, ln_mix[l], w_in, col_start=0, n_cols=qk_width, out_dtype=F32)
            v = norm_matmul(xt, ln_mix[l], w_in, col_start=qk_width, n_cols=v_width, out_dtype=BF16)
            g = norm_matmul(xt, ln_mix[l], w_in, col_start=qk_width + v_width, n_cols=v_width, out_dtype=F32)
            y = retention_core(qk, v, g, ret_gn[l], batch, seq)
            xt = matmul_residual(y, ret_w_out[l].astype(BF16), xt)
        else:
            j = l - n_a
            q = norm_matmul(xt, ln_mix[l], att_w_q[j].astype(BF16), col_start=0, n_cols=d, out_dtype=BF16,
                            scale=ATT_HEAD_DIM ** -0.5)
            a = attention_core(q, kv, _attention_bias(att_rel_bias[j]), batch, seq)
            xt = matmul_residual(a, att_w_o[j].astype(BF16), xt)
        xt = peer_layer(xt, ln_ffn[l], peer_w_q[l].T.astype(BF16), peer_sub_keys[l].astype(BF16),
                        peer_u[l].astype(BF16), peer_v[l].T.astype(BF16))
        if l == n_a - 1:
            kv = norm_matmul(xt, kv_norm, w_kv.astype(BF16), col_start=0, n_cols=2 * d, out_dtype=BF16)
    return final_norm(xt, ln_final).reshape(batch, seq, d)
```

```python
import functools

import jax
import jax.numpy as jnp
from jax import lax
from jax.experimental import pallas as pl
from jax.experimental.pallas import tpu as pltpu

F32 = jnp.float32
BF16 = jnp.bfloat16

EPS = 1e-6
ROPE_BASE = 10000.0
REF_CHUNK = 64
RET_HEADS = 8
RET_QK_DIM = 256
RET_V_DIM = 512
RET_BLOCK = 512
ATT_HEADS = 16
ATT_HEAD_DIM = 128
LEFT_CHUNKS = 8
REL_CLIP = 128
ATT_QBLOCK = 256
ATT_WINDOW = ATT_QBLOCK + LEFT_CHUNKS * REF_CHUNK
ATT_HEAD_UNROLL = 4
PEER_HEADS = 8
PEER_NKEYS = 128
PEER_HALF = 128
PEER_TOPK = 16
MASK_VALUE = -1e30
NEG_INF = float("-inf")

V7X_LANES = 128
V7X_VMEM_LIMIT_BYTES = 60 * 1024 * 1024

NT_DIMS = (((1,), (1,)), ((), ()))
TN_DIMS = (((0,), (0,)), ((), ()))


def _params(*sem, flags=None):
    return pltpu.CompilerParams(dimension_semantics=sem, vmem_limit_bytes=V7X_VMEM_LIMIT_BYTES, flags=flags)


def _rmsnorm_rows(x, g):
    ms = jnp.mean(x * x, axis=-1, keepdims=True)
    return x * lax.rsqrt(ms + EPS) * g


def _norm_matmul_kernel(x_ref, g_ref, w_ref, o_ref, hn_ref, *, scale):
    @pl.when(pl.program_id(1) == 0)
    def _():
        hn_ref[...] = _rmsnorm_rows(x_ref[...], g_ref[...]).astype(BF16)

    acc = jnp.dot(hn_ref[...], w_ref[...], preferred_element_type=F32)
    if scale != 1.0:
        acc = acc * scale
    o_ref[...] = acc.astype(o_ref.dtype)


def norm_matmul(x, gain, w, *, col_start, n_cols, out_dtype, scale=1.0, tm=1024, tn=1024):
    t, d = x.shape
    tm = min(tm, t)
    tn = min(tn, n_cols)
    assert t % tm == 0 and n_cols % tn == 0 and col_start % tn == 0
    off = col_start // tn
    return pl.pallas_call(
        functools.partial(_norm_matmul_kernel, scale=scale),
        grid=(t // tm, n_cols // tn),
        in_specs=[
            pl.BlockSpec((tm, d), lambda i, j: (i, 0)),
            pl.BlockSpec((1, d), lambda i, j: (0, 0)),
            pl.BlockSpec((d, tn), lambda i, j: (0, j + off)),
        ],
        out_specs=pl.BlockSpec((tm, tn), lambda i, j: (i, j)),
        out_shape=jax.ShapeDtypeStruct((t, n_cols), out_dtype),
        scratch_shapes=[pltpu.VMEM((tm, d), BF16)],
        compiler_params=_params("parallel", "arbitrary"),
        name="norm_matmul",
    )(x, gain.reshape(1, d), w)


def _matmul_residual_kernel(y_ref, w_ref, x_ref, o_ref):
    o_ref[...] = x_ref[...] + jnp.dot(y_ref[...], w_ref[...], preferred_element_type=F32)


def matmul_residual(y, w, x, *, tm=1024, tn=512):
    t, k = y.shape
    n = w.shape[1]
    tm = min(tm, t)
    tn = min(tn, n)
    assert t % tm == 0 and n % tn == 0
    return pl.pallas_call(
        _matmul_residual_kernel,
        grid=(t // tm, n // tn),
        in_specs=[
            pl.BlockSpec((tm, k), lambda i, j: (i, 0)),
            pl.BlockSpec((k, tn), lambda i, j: (0, j)),
            pl.BlockSpec((tm, tn), lambda i, j: (i, j)),
        ],
        out_specs=pl.BlockSpec((tm, tn), lambda i, j: (i, j)),
        out_shape=jax.ShapeDtypeStruct((t, n), F32),
        compiler_params=_params("parallel", "arbitrary"),
        name="matmul_residual",
    )(y, w, x)


def _retention_tables(seq):
    c = RET_BLOCK
    log_gamma = jnp.log1p(-jnp.exp2(-5.0 - jnp.arange(RET_HEADS, dtype=F32)))
    idx = jnp.arange(c, dtype=F32)
    diff = idx[:, None] - idx[None, :]
    dmask = jnp.where(diff >= 0, jnp.exp(log_gamma[:, None, None] * jnp.maximum(diff, 0.0)), 0.0)
    xi = jnp.exp(log_gamma[:, None] * (idx[None, :] + 1.0))
    zeta = jnp.exp(log_gamma[:, None] * (c - 1.0 - idx[None, :]))
    cdecay = jnp.exp(log_gamma * c)
    xi = jnp.broadcast_to(xi[:, :, None], (RET_HEADS, c, V7X_LANES))
    zeta = jnp.broadcast_to(zeta[:, :, None], (RET_HEADS, c, V7X_LANES))
    cdecay = jnp.broadcast_to(cdecay[:, None, None], (RET_HEADS, 1, RET_V_DIM))
    half = RET_QK_DIM // 2
    inv = 1.0 / (ROPE_BASE ** (jnp.arange(half, dtype=F32) / half))
    ang = jnp.arange(seq, dtype=F32)[:, None] * inv[None, :]
    return dmask, xi, zeta, cdecay, jnp.cos(ang), jnp.sin(ang)


def _retention_kernel(q_ref, k_ref, v_ref, g_ref, cos_ref, sin_ref, dmask_ref, xi_ref, zeta_ref, cd_ref,
                      gn_ref, y_ref, state_ref):
    @pl.when(pl.program_id(2) == 0)
    def _():
        state_ref[...] = jnp.zeros_like(state_ref)

    cos = cos_ref[...]
    sin = sin_ref[...]
    half = RET_QK_DIM // 2

    def rotate(t):
        t1 = t[:, :half]
        t2 = t[:, half:]
        return jnp.concatenate([t1 * cos - t2 * sin, t1 * sin + t2 * cos], axis=-1)

    q = rotate(q_ref[...])
    k = rotate(k_ref[...]) * (RET_QK_DIM ** -0.5)
    qb = q.astype(BF16)
    kb = k.astype(BF16)
    v = v_ref[...]

    scores = lax.dot_general(qb, kb, NT_DIMS, preferred_element_type=F32) * dmask_ref[...]
    intra = jnp.dot(scores.astype(BF16), v, preferred_element_type=F32)
    state = state_ref[...]
    xi = jnp.concatenate([xi_ref[...]] * (RET_V_DIM // V7X_LANES), axis=-1)
    cross = jnp.dot(qb, state.astype(BF16), preferred_element_type=F32) * xi
    o = intra + cross

    zeta = jnp.concatenate([zeta_ref[...]] * (RET_QK_DIM // V7X_LANES), axis=-1)
    kz = (k * zeta).astype(BF16)
    state_ref[...] = state * cd_ref[...] + lax.dot_general(kz, v, TN_DIMS, preferred_element_type=F32)

    mu = jnp.mean(o, axis=-1, keepdims=True)
    d = o - mu
    var = jnp.mean(d * d, axis=-1, keepdims=True)
    y = d * lax.rsqrt(var + EPS) * gn_ref[...]
    g = g_ref[...]
    y_ref[...] = (g * jax.nn.sigmoid(g) * y).astype(y_ref.dtype)


def retention_core(qk, v, g, gn, batch, seq):
    c = RET_BLOCK
    nc = seq // c
    dmask, xi, zeta, cdecay, cos, sin = _retention_tables(seq)
    row = lambda b, h, n: (b * nc + n, h)
    per_head = lambda b, h, n: (h, 0, 0)
    return pl.pallas_call(
        _retention_kernel,
        grid=(batch, RET_HEADS, nc),
        in_specs=[
            pl.BlockSpec((c, RET_QK_DIM), row),
            pl.BlockSpec((c, RET_QK_DIM), lambda b, h, n: (b * nc + n, RET_HEADS + h)),
            pl.BlockSpec((c, RET_V_DIM), row),
            pl.BlockSpec((c, RET_V_DIM), row),
            pl.BlockSpec((c, RET_QK_DIM // 2), lambda b, h, n: (n, 0)),
            pl.BlockSpec((c, RET_QK_DIM // 2), lambda b, h, n: (n, 0)),
            pl.BlockSpec((None, c, c), per_head),
            pl.BlockSpec((None, c, V7X_LANES), per_head),
            pl.BlockSpec((None, c, V7X_LANES), per_head),
            pl.BlockSpec((None, 1, RET_V_DIM), per_head),
            pl.BlockSpec((1, RET_V_DIM), lambda b, h, n: (0, h)),
        ],
        out_specs=pl.BlockSpec((c, RET_V_DIM), row),
        out_shape=jax.ShapeDtypeStruct(v.shape, BF16),
        scratch_shapes=[pltpu.VMEM((RET_QK_DIM, RET_V_DIM), F32)],
        compiler_params=_params("parallel", "parallel", "arbitrary"),
        name="retention_core",
    )(qk, qk, v, g, cos, sin, dmask, xi, zeta, cdecay, gn.reshape(1, -1))


def _attention_bias(rel_bias):
    i = jnp.arange(ATT_QBLOCK)[:, None]
    m = jnp.arange(ATT_WINDOW)[None, :]
    period = ATT_QBLOCK + ATT_WINDOW - 1
    lag = jnp.arange(period)
    lag = jnp.where(lag < ATT_WINDOW, lag, lag - period)
    rel = jnp.clip(LEFT_CHUNKS * REF_CHUNK - lag, -REL_CLIP, REL_CLIP) + REL_CLIP
    first = rel_bias.astype(F32)[:, rel]
    heads = rel_bias.shape[0]
    skew = jnp.tile(first, (1, ATT_QBLOCK + 1))[:, :ATT_QBLOCK * (period - 1)]
    toeplitz = skew.reshape(heads, ATT_QBLOCK, period - 1)[:, :, :ATT_WINDOW]
    ci = i // REF_CHUNK
    cm = m // REF_CHUNK
    band = (cm >= ci) & (cm <= ci + LEFT_CHUNKS)
    return jnp.where(band[None], toeplitz, MASK_VALUE)


def _attention_kernel(q_ref, k0_ref, k1_ref, k2_ref, v0_ref, v1_ref, v2_ref, bias_ref, o_ref):
    qblk = pl.program_id(1)
    n_prev = LEFT_CHUNKS * REF_CHUNK // ATT_QBLOCK
    col = lax.broadcasted_iota(jnp.int32, (1, ATT_WINDOW), 1)
    start_mask = jnp.where(col >= (n_prev - qblk) * ATT_QBLOCK, 0.0, MASK_VALUE)
    k_refs = (k0_ref, k1_ref, k2_ref)
    v_refs = (v0_ref, v1_ref, v2_ref)

    def one_head(h, carry):
        cols = pl.ds(pl.multiple_of(h * ATT_HEAD_DIM, ATT_HEAD_DIM), ATT_HEAD_DIM)
        q = q_ref[:, cols]
        s = jnp.concatenate(
            [lax.dot_general(q, kr[:, cols], NT_DIMS, preferred_element_type=F32) for kr in k_refs], axis=-1)
        s = s + bias_ref[h] + start_mask
        m = jnp.max(s, axis=-1, keepdims=True)
        p = jnp.exp(s - m)
        p = (p / jnp.sum(p, axis=-1, keepdims=True)).astype(BF16)
        o = jnp.dot(p[:, :ATT_QBLOCK], v_refs[0][:, cols], preferred_element_type=F32)
        for w in range(1, len(v_refs)):
            o = o + jnp.dot(p[:, w * ATT_QBLOCK:(w + 1) * ATT_QBLOCK], v_refs[w][:, cols],
                            preferred_element_type=F32)
        o_ref[:, cols] = o.astype(o_ref.dtype)
        return carry

    lax.fori_loop(0, ATT_HEADS, one_head, 0, unroll=ATT_HEAD_UNROLL)


def attention_core(q, kv, bias, batch, seq):
    t, d = q.shape
    nq = seq // ATT_QBLOCK
    n_prev = LEFT_CHUNKS * REF_CHUNK // ATT_QBLOCK
    assert ATT_WINDOW == (n_prev + 1) * ATT_QBLOCK

    def window(w, part):
        return pl.BlockSpec((ATT_QBLOCK, d), lambda b, n: (b * nq + jnp.maximum(n - n_prev + w, 0), part))

    return pl.pallas_call(
        _attention_kernel,
        grid=(batch, nq),
        in_specs=[pl.BlockSpec((ATT_QBLOCK, d), lambda b, n: (b * nq + n, 0))]
        + [window(w, 0) for w in range(n_prev + 1)]
        + [window(w, 1) for w in range(n_prev + 1)]
        + [pl.BlockSpec(bias.shape, lambda b, n: (0, 0, 0))],
        out_specs=pl.BlockSpec((ATT_QBLOCK, d), lambda b, n: (b * nq + n, 0)),
        out_shape=jax.ShapeDtypeStruct((t, d), BF16),
        compiler_params=_params("parallel", "arbitrary"),
        name="attention_core",
    )(q, kv, kv, kv, kv, kv, kv, bias)


PEER_RANKS = PEER_TOPK + 1
V7X_SUBLANES = 8


def _sorting_network(n):
    pairs = []
    p = 1
    while p < n:
        k = p
        while k >= 1:
            for j in range(k % p, n - k, 2 * k):
                for i in range(min(k, n - j - k)):
                    if (i + j) // (2 * p) == (i + j + k) // (2 * p):
                        pairs.append((i + j, i + j + k))
            k //= 2
        p *= 2
    return pairs


def _sort_descending(vals):
    vals = list(vals)
    for i, j in _sorting_network(len(vals)):
        vals[i], vals[j] = jnp.maximum(vals[i], vals[j]), jnp.minimum(vals[i], vals[j])
    return vals


def _top_ranked(s):
    n = PEER_TOPK
    tiles = [s[r * V7X_SUBLANES:(r + 1) * V7X_SUBLANES, :] for r in range(s.shape[0] // V7X_SUBLANES)]
    assert len(tiles) == n
    top = _sort_descending(tiles)
    shift = V7X_SUBLANES // 2
    while shift >= 1:
        other = [pltpu.roll(v, shift, 0) for v in top]
        top = [jnp.maximum(top[i], other[n - 1 - i]) for i in range(n)]
        k = n // 2
        while k >= 1:
            for i in range(n):
                if i & k == 0:
                    top[i], top[i + k] = jnp.maximum(top[i], top[i + k]), jnp.minimum(top[i], top[i + k])
            k //= 2
        shift //= 2
    below = [jnp.where(t < top[n - 1], t, NEG_INF) for t in tiles]
    while len(below) > 1:
        below = [jnp.maximum(below[2 * i], below[2 * i + 1]) for i in range(len(below) // 2)]
    nxt = below[0]
    shift = V7X_SUBLANES // 2
    while shift >= 1:
        nxt = jnp.maximum(nxt, pltpu.roll(nxt, shift, 0))
        shift //= 2
    return top + [nxt]


def _bf16_twice(x):
    hi = pltpu.bitcast(x.astype(BF16).astype(F32), jnp.uint32)
    return hi | (hi >> 16)


def _peer_prep_kernel(x_ref, g_ref, wqt_ref, sk_ref, hnt_ref, b2_ref, e2_ref, n1_ref, c_ref, qt_ref):
    tm = x_ref.shape[0]
    lane_groups = tm // V7X_LANES
    assert lane_groups <= V7X_SUBLANES
    hnt = _rmsnorm_rows(x_ref[...], g_ref[...]).T.astype(BF16)
    hnt_ref[...] = hnt
    qt_ref[...] = jnp.dot(wqt_ref[...], hnt, preferred_element_type=F32)
    sublane = lax.broadcasted_iota(jnp.int32, (V7X_SUBLANES, V7X_LANES), 0)

    def one_head(h, carry):
        scores = []
        ranked = []
        for part in range(2):
            rows = pl.ds(pl.multiple_of(h * (2 * PEER_HALF) + part * PEER_HALF, PEER_HALF), PEER_HALF)
            s = jnp.dot(sk_ref[part], qt_ref[rows, :].astype(BF16), preferred_element_type=F32)
            scores.append(s)
            per_group = [_top_ranked(s[:, g * V7X_LANES:(g + 1) * V7X_LANES]) for g in range(lane_groups)]
            dense = []
            for r in range(PEER_RANKS):
                tile = per_group[lane_groups - 1][r]
                for g in range(lane_groups - 2, -1, -1):
                    tile = jnp.where(sublane == g, per_group[g][r], tile)
                dense.append(tile)
            ranked.append(dense)

        xs, ys = ranked
        sums = [xs[i] + ys[j] for i in range(PEER_RANKS) for j in range(PEER_RANKS) if (i + 1) * (j + 1) <= PEER_RANKS]
        size = 1
        while size < len(sums):
            size *= 2
        sums = _sort_descending(sums + [jnp.full_like(sums[0], NEG_INF)] * (size - len(sums)))
        z = jnp.ones_like(sums[0])
        for v in sums[1:PEER_TOPK]:
            z = z + jnp.exp(v - sums[0])
        tau = 0.5 * (sums[PEER_TOPK - 1] + sums[PEER_TOPK])
        half_over_z = 0.5 / z

        s1, s2 = scores
        for g in range(lane_groups):
            lanes = slice(g * V7X_LANES, (g + 1) * V7X_LANES)
            row = lambda t, g=g: t[g:g + 1, :]
            s1_g = s1[:, lanes]
            s2_g = s2[:, lanes]
            thr = row(tau) - s1_g
            b2 = jnp.zeros_like(s2_g)
            n1 = jnp.zeros_like(s1_g)
            for r in range(PEER_TOPK):
                y = row(ys[r])
                b2 = b2 + jnp.where(y > s2_g, 1.0, 0.0)
                n1 = n1 + jnp.where(y >= thr, 1.0, 0.0)
            c = jnp.exp(s1_g - row(xs[0])) * row(half_over_z)
            n1_ref[h, :, lanes] = _bf16_twice(n1)
            c_ref[h, :, lanes] = _bf16_twice(c)
            b2_ref[h, g] = pltpu.bitcast(b2.astype(BF16), jnp.uint32)
            e2_ref[h, g] = pltpu.bitcast(jnp.exp(s2_g - row(ys[0])).astype(BF16), jnp.uint32)
        return carry

    lax.fori_loop(0, PEER_HEADS, one_head, 0)


def peer_prep(x, gain, wq_t, sub_keys, *, tm=512):
    t, d = x.shape
    tm = min(tm, t)
    assert t % tm == 0 and tm % V7X_LANES == 0
    nq = wq_t.shape[0]
    fac = jax.ShapeDtypeStruct((PEER_HEADS, PEER_NKEYS, t), jnp.uint32)
    fac_spec = pl.BlockSpec((PEER_HEADS, PEER_NKEYS, tm), lambda i: (0, 0, i))
    blocked = jax.ShapeDtypeStruct((PEER_HEADS, t // V7X_LANES, PEER_NKEYS // 2, V7X_LANES), jnp.uint32)
    blocked_spec = pl.BlockSpec((PEER_HEADS, tm // V7X_LANES, PEER_NKEYS // 2, V7X_LANES), lambda i: (0, i, 0, 0))
    return pl.pallas_call(
        _peer_prep_kernel,
        grid=(t // tm,),
        in_specs=[
            pl.BlockSpec((tm, d), lambda i: (i, 0)),
            pl.BlockSpec((1, d), lambda i: (0, 0)),
            pl.BlockSpec((nq, d), lambda i: (0, 0)),
            pl.BlockSpec(sub_keys.shape, lambda i: (0, 0, 0)),
        ],
        out_specs=[pl.BlockSpec((d, tm), lambda i: (0, i)), blocked_spec, blocked_spec, fac_spec, fac_spec],
        out_shape=[jax.ShapeDtypeStruct((d, t), BF16), blocked, blocked, fac, fac],
        scratch_shapes=[pltpu.VMEM((nq, tm), F32)],
        compiler_params=_params("parallel"),
        name="peer_prep",
    )(x, gain.reshape(1, d), wq_t, sub_keys)


PEER_KEYS_PER_STEP = 8
PEER_ETILE = PEER_KEYS_PER_STEP * PEER_NKEYS
PEER_SUBROWS = 16


def _peer_main_kernel(hnt_first_ref, u_first_ref, hnt_ref, u_ref, vt_ref, b2_ref, e2_ref, n1_ref, c_ref, x_ref,
                      o_ref, acc_ref, at_ref, p_ref, *, tiles_per_block):
    g = pl.program_id(0)
    j = lax.rem(g, tiles_per_block)
    tm = hnt_ref.shape[1]
    lane_groups = tm // V7X_LANES

    halves = 2
    keys_per_half = PEER_KEYS_PER_STEP // halves
    rows_per_half = PEER_ETILE // halves
    words = PEER_SUBROWS // 2
    word_tile = (words, V7X_LANES)

    def packed(x):
        return pltpu.bitcast(x, BF16)

    def pre_activation(u_tile_ref, hn_ref, slot):
        at_ref[slot] = jnp.dot(u_tile_ref[...], hn_ref[...], preferred_element_type=F32)

    @pl.when(g == 0)
    def _():
        pre_activation(u_first_ref, hnt_first_ref, 0)

    @pl.when(j == 0)
    def _():
        acc_ref[...] = jnp.zeros_like(acc_ref)

    def weighted_gelu(e, slot):
        for ka in range(e * keys_per_half, (e + 1) * keys_per_half):
            for lg in range(lane_groups):
                lanes = slice(lg * V7X_LANES, (lg + 1) * V7X_LANES)
                n1 = [packed(jnp.broadcast_to(n1_ref[h, ka:ka + 1, lanes], word_tile)) for h in range(PEER_HEADS)]
                cw = [packed(jnp.broadcast_to(c_ref[h, ka:ka + 1, lanes], word_tile)) for h in range(PEER_HEADS)]
                for bg in range(PEER_NKEYS // PEER_SUBROWS):
                    wrows = slice(bg * words, (bg + 1) * words)
                    w = None
                    for h in range(PEER_HEADS):
                        term = jnp.where(packed(b2_ref[h, lg, wrows, :]) < n1[h],
                                         packed(e2_ref[h, lg, wrows, :]) * cw[h], 0.0)
                        w = term if w is None else w + term
                    first = ka * PEER_NKEYS + bg * PEER_SUBROWS
                    pre = at_ref[slot, first:first + PEER_SUBROWS, lanes]
                    act = pre * (1.0 + lax.erf(pre * (2.0 ** -0.5)))
                    p_ref[lg, first // 2:first // 2 + words, :] = pltpu.bitcast(act.astype(BF16) * w, jnp.uint32)

    def value_matmul(e):
        rows = slice(e * rows_per_half, (e + 1) * rows_per_half)
        wrows = slice(e * rows_per_half // 2, (e + 1) * rows_per_half // 2)
        p_half = jnp.concatenate([packed(p_ref[lg, wrows, :]) for lg in range(lane_groups)], axis=-1)
        acc_ref[...] += jnp.dot(vt_ref[:, rows], p_half, preferred_element_type=F32)

    def step(slot):
        pre_activation(u_ref, hnt_ref, 1 - slot)
        weighted_gelu(0, slot)
        value_matmul(0)
        weighted_gelu(1, slot)
        value_matmul(1)

    for parity in range(2):
        pl.when(lax.rem(g, 2) == parity)(functools.partial(step, parity))

    @pl.when(j == tiles_per_block - 1)
    def _():
        o_ref[...] = x_ref[...] + acc_ref[...].T


def peer_main(hn_t, u, v_t, b2, e2, n1, c, x, *, tm=512):
    t, d = x.shape
    n_exp = u.shape[0]
    assert v_t.shape == (n_exp // PEER_ETILE, d, PEER_ETILE)
    tm = min(tm, t)
    assert t % tm == 0 and tm % V7X_LANES == 0
    assert n_exp % PEER_ETILE == 0 and n_exp == PEER_NKEYS * PEER_NKEYS
    lane_groups = tm // V7X_LANES
    n_i = t // tm
    n_j = n_exp // PEER_ETILE
    assert n_j % 2 == 0

    def block(g):
        return g // n_j

    def tile(g):
        return lax.rem(g, n_j)

    def nxt(g):
        return jnp.minimum(g + 1, n_i * n_j - 1)

    all_keys = pl.BlockSpec((PEER_HEADS, lane_groups, PEER_NKEYS // 2, V7X_LANES), lambda g: (0, block(g), 0, 0))
    step_keys = pl.BlockSpec((PEER_HEADS, PEER_KEYS_PER_STEP, tm), lambda g: (0, tile(g), block(g)))
    return pl.pallas_call(
        functools.partial(_peer_main_kernel, tiles_per_block=n_j),
        grid=(n_i * n_j,),
        in_specs=[
            pl.BlockSpec((d, tm), lambda g: (0, 0), pipeline_mode=pl.Buffered(1)),
            pl.BlockSpec((PEER_ETILE, d), lambda g: (0, 0), pipeline_mode=pl.Buffered(1)),
            pl.BlockSpec((d, tm), lambda g: (0, block(nxt(g)))),
            pl.BlockSpec((PEER_ETILE, d), lambda g: (tile(nxt(g)), 0)),
            pl.BlockSpec((None, d, PEER_ETILE), lambda g: (tile(g), 0, 0)),
            all_keys, all_keys, step_keys, step_keys,
            pl.BlockSpec((tm, d), lambda g: (block(g), 0)),
        ],
        out_specs=pl.BlockSpec((tm, d), lambda g: (block(g), 0)),
        out_shape=jax.ShapeDtypeStruct((t, d), F32),
        scratch_shapes=[
            pltpu.VMEM((d, tm), F32),
            pltpu.VMEM((2, PEER_ETILE, tm), F32),
            pltpu.VMEM((lane_groups, PEER_ETILE // 2, V7X_LANES), jnp.uint32),
        ],
        compiler_params=_params("arbitrary"),
        name="peer_main",
    )(hn_t, u, hn_t, u, v_t, b2, e2, n1, c, x)


def peer_layer(x, gain, wq_t, sub_keys, u, v_t):
    hn_t, b2, e2, n1, c = peer_prep(x, gain, wq_t, sub_keys)
    return peer_main(hn_t, u, v_t, b2, e2, n1, c, x)


def _final_norm_kernel(x_ref, g_ref, o_ref):
    o_ref[...] = _rmsnorm_rows(x_ref[...], g_ref[...])


def final_norm(x, gain, *, tm=1024):
    t, d = x.shape
    tm = min(tm, t)
    return pl.pallas_call(
        _final_norm_kernel,
        grid=(t // tm,),
        in_specs=[pl.BlockSpec((tm, d), lambda i: (i, 0)), pl.BlockSpec((1, d), lambda i: (0, 0))],
        out_specs=pl.BlockSpec((tm, d), lambda i: (i, 0)),
        out_shape=jax.ShapeDtypeStruct((t, d), F32),
        compiler_params=_params("parallel"),
        name="final_norm",
    )(x, gain.reshape(1, d))


def kernel(x, ln_mix, ln_ffn, ret_w_in, ret_w_out, ret_gn, kv_norm, w_kv, att_w_q, att_w_o, att_rel_bias,
           peer_w_q, peer_sub_keys, peer_u, peer_v, ln_final):
    batch, seq, d = x.shape
    depth = ln_mix.shape[0]
    n_a = ret_w_in.shape[0]
    xt = x.reshape(batch * seq, d)
    qk_width = 2 * RET_HEADS * RET_QK_DIM
    v_width = RET_HEADS * RET_V_DIM
    kv = None
    for l in range(depth):
        if l < n_a:
            w_in = ret_w_in[l].astype(BF16)
            qk = norm_matmul(xt, ln_mix[l], w_in, col_start=0, n_cols=qk_width, out_dtype=F32)
            v = norm_matmul(xt, ln_mix[l], w_in, col_start=qk_width, n_cols=v_width, out_dtype=BF16)
            g = norm_matmul(xt, ln_mix[l], w_in, col_start=qk_width + v_width, n_cols=v_width, out_dtype=F32)
            y = retention_core(qk, v, g, ret_gn[l], batch, seq)
            xt = matmul_residual(y, ret_w_out[l].astype(BF16), xt)
        else:
            j = l - n_a
            q = norm_matmul(xt, ln_mix[l], att_w_q[j].astype(BF16), col_start=0, n_cols=d, out_dtype=BF16,
                            scale=ATT_HEAD_DIM ** -0.5)
            a = attention_core(q, kv, _attention_bias(att_rel_bias[j]), batch, seq)
            xt = matmul_residual(a, att_w_o[j].astype(BF16), xt)
        xt = peer_layer(xt, ln_ffn[l], peer_w_q[l].T.astype(BF16), peer_sub_keys[l].astype(BF16),
                        peer_u[l].astype(BF16),
                        peer_v[l].reshape(-1, PEER_ETILE, d).transpose(0, 2, 1).astype(BF16))
        if l == n_a - 1:
            kv = norm_matmul(xt, kv_norm, w_kv.astype(BF16), col_start=0, n_cols=2 * d, out_dtype=BF16)
    return final_norm(xt, ln_final).reshape(batch, seq, d)
```

```python
import functools

import jax
import jax.numpy as jnp
from jax import lax
from jax.experimental import pallas as pl
from jax.experimental.pallas import tpu as pltpu

F32 = jnp.float32
BF16 = jnp.bfloat16

EPS = 1e-6
ROPE_BASE = 10000.0
REF_CHUNK = 64
RET_HEADS = 8
RET_QK_DIM = 256
RET_V_DIM = 512
RET_BLOCK = 512
ATT_HEADS = 16
ATT_HEAD_DIM = 128
LEFT_CHUNKS = 8
REL_CLIP = 128
ATT_QBLOCK = 256
ATT_WINDOW = ATT_QBLOCK + LEFT_CHUNKS * REF_CHUNK
ATT_HEAD_UNROLL = 4
PEER_HEADS = 8
PEER_NKEYS = 128
PEER_HALF = 128
PEER_TOPK = 16
MASK_VALUE = -1e30
NEG_INF = float("-inf")

V7X_LANES = 128
V7X_VMEM_LIMIT_BYTES = 60 * 1024 * 1024

NT_DIMS = (((1,), (1,)), ((), ()))
TN_DIMS = (((0,), (0,)), ((), ()))


def _params(*sem, flags=None):
    return pltpu.CompilerParams(dimension_semantics=sem, vmem_limit_bytes=V7X_VMEM_LIMIT_BYTES, flags=flags)


def _rmsnorm_rows(x, g):
    ms = jnp.mean(x * x, axis=-1, keepdims=True)
    return x * lax.rsqrt(ms + EPS) * g


def _norm_matmul_kernel(x_ref, g_ref, w_ref, o_ref, hn_ref, *, scale):
    @pl.when(pl.program_id(1) == 0)
    def _():
        hn_ref[...] = _rmsnorm_rows(x_ref[...], g_ref[...]).astype(BF16)

    acc = jnp.dot(hn_ref[...], w_ref[...], preferred_element_type=F32)
    if scale != 1.0:
        acc = acc * scale
    o_ref[...] = acc.astype(o_ref.dtype)


def norm_matmul(x, gain, w, *, col_start, n_cols, out_dtype, scale=1.0, tm=1024, tn=1024):
    t, d = x.shape
    tm = min(tm, t)
    tn = min(tn, n_cols)
    assert t % tm == 0 and n_cols % tn == 0 and col_start % tn == 0
    off = col_start // tn
    return pl.pallas_call(
        functools.partial(_norm_matmul_kernel, scale=scale),
        grid=(t // tm, n_cols // tn),
        in_specs=[
            pl.BlockSpec((tm, d), lambda i, j: (i, 0)),
            pl.BlockSpec((1, d), lambda i, j: (0, 0)),
            pl.BlockSpec((d, tn), lambda i, j: (0, j + off)),
        ],
        out_specs=pl.BlockSpec((tm, tn), lambda i, j: (i, j)),
        out_shape=jax.ShapeDtypeStruct((t, n_cols), out_dtype),
        scratch_shapes=[pltpu.VMEM((tm, d), BF16)],
        compiler_params=_params("parallel", "arbitrary"),
        name="norm_matmul",
    )(x, gain.reshape(1, d), w)


def _norm_matmul_split_kernel(x_ref, g_ref, w_ref, *refs, tile_bounds):
    out_refs, hn_ref = refs[:-1], refs[-1]
    j = pl.program_id(1)

    @pl.when(j == 0)
    def _():
        hn_ref[...] = _rmsnorm_rows(x_ref[...], g_ref[...]).astype(BF16)

    for o_ref, (lo, hi) in zip(out_refs, tile_bounds):
        @pl.when((j >= lo) & (j < hi))
        def _(o_ref=o_ref):
            o_ref[...] = jnp.dot(hn_ref[...], w_ref[...], preferred_element_type=F32).astype(o_ref.dtype)


def norm_matmul_split(x, gain, w, layer, widths, dtypes, *, tm=1024, tn=1024):
    t, d = x.shape
    tm = min(tm, t)
    assert t % tm == 0 and all(width % tn == 0 for width in widths) and sum(widths) == w.shape[-1]
    tile_bounds = []
    lo = 0
    for width in widths:
        tile_bounds.append((lo, lo + width // tn))
        lo += width // tn

    def out_spec(lo, hi):
        return pl.BlockSpec((tm, tn), lambda i, j: (i, jnp.clip(j - lo, 0, hi - lo - 1)))

    return pl.pallas_call(
        functools.partial(_norm_matmul_split_kernel, tile_bounds=tuple(tile_bounds)),
        grid=(t // tm, lo),
        in_specs=[
            pl.BlockSpec((tm, d), lambda i, j: (i, 0)),
            pl.BlockSpec((1, d), lambda i, j: (0, 0)),
            pl.BlockSpec((None, d, tn), lambda i, j: (layer, 0, j)),
        ],
        out_specs=[out_spec(*b) for b in tile_bounds],
        out_shape=[jax.ShapeDtypeStruct((t, width), dt) for width, dt in zip(widths, dtypes)],
        scratch_shapes=[pltpu.VMEM((tm, d), BF16)],
        compiler_params=_params("parallel", "arbitrary"),
        name="norm_matmul_split",
    )(x, gain.reshape(1, d), w)


def _matmul_residual_kernel(y_ref, w_ref, x_ref, o_ref):
    o_ref[...] = x_ref[...] + jnp.dot(y_ref[...], w_ref[...], preferred_element_type=F32)


def matmul_residual(y, w, x, *, tm=1024, tn=512):
    t, k = y.shape
    n = w.shape[1]
    tm = min(tm, t)
    tn = min(tn, n)
    assert t % tm == 0 and n % tn == 0
    return pl.pallas_call(
        _matmul_residual_kernel,
        grid=(t // tm, n // tn),
        in_specs=[
            pl.BlockSpec((tm, k), lambda i, j: (i, 0)),
            pl.BlockSpec((k, tn), lambda i, j: (0, j)),
            pl.BlockSpec((tm, tn), lambda i, j: (i, j)),
        ],
        out_specs=pl.BlockSpec((tm, tn), lambda i, j: (i, j)),
        out_shape=jax.ShapeDtypeStruct((t, n), F32),
        compiler_params=_params("parallel", "arbitrary"),
        name="matmul_residual",
    )(y, w, x)


def _retention_tables(seq):
    c = RET_BLOCK
    log_gamma = jnp.log1p(-jnp.exp2(-5.0 - jnp.arange(RET_HEADS, dtype=F32)))
    idx = jnp.arange(c, dtype=F32)
    diff = idx[:, None] - idx[None, :]
    dmask = jnp.where(diff >= 0, jnp.exp(log_gamma[:, None, None] * jnp.maximum(diff, 0.0)), 0.0)
    xi = jnp.exp(log_gamma[:, None] * (idx[None, :] + 1.0))
    zeta = jnp.exp(log_gamma[:, None] * (c - 1.0 - idx[None, :]))
    cdecay = jnp.exp(log_gamma * c)
    xi = jnp.broadcast_to(xi[:, :, None], (RET_HEADS, c, V7X_LANES))
    zeta = jnp.broadcast_to(zeta[:, :, None], (RET_HEADS, c, V7X_LANES))
    cdecay = jnp.broadcast_to(cdecay[:, None, None], (RET_HEADS, 1, RET_V_DIM))
    half = RET_QK_DIM // 2
    inv = 1.0 / (ROPE_BASE ** (jnp.arange(half, dtype=F32) / half))
    ang = jnp.arange(seq, dtype=F32)[:, None] * inv[None, :]
    return dmask, xi, zeta, cdecay, jnp.cos(ang), jnp.sin(ang)


def _retention_kernel(q_ref, k_ref, v_ref, g_ref, cos_ref, sin_ref, dmask_ref, xi_ref, zeta_ref, cd_ref,
                      gn_ref, y_ref, state_ref):
    @pl.when(pl.program_id(2) == 0)
    def _():
        state_ref[...] = jnp.zeros_like(state_ref)

    cos = cos_ref[...]
    sin = sin_ref[...]
    half = RET_QK_DIM // 2

    def rotate(t):
        t1 = t[:, :half]
        t2 = t[:, half:]
        return jnp.concatenate([t1 * cos - t2 * sin, t1 * sin + t2 * cos], axis=-1)

    q = rotate(q_ref[...])
    k = rotate(k_ref[...]) * (RET_QK_DIM ** -0.5)
    qb = q.astype(BF16)
    kb = k.astype(BF16)
    v = v_ref[...]

    scores = lax.dot_general(qb, kb, NT_DIMS, preferred_element_type=F32) * dmask_ref[...]
    intra = jnp.dot(scores.astype(BF16), v, preferred_element_type=F32)
    state = state_ref[...]
    xi = jnp.concatenate([xi_ref[...]] * (RET_V_DIM // V7X_LANES), axis=-1)
    cross = jnp.dot(qb, state.astype(BF16), preferred_element_type=F32) * xi
    o = intra + cross

    zeta = jnp.concatenate([zeta_ref[...]] * (RET_QK_DIM // V7X_LANES), axis=-1)
    kz = (k * zeta).astype(BF16)
    state_ref[...] = state * cd_ref[...] + lax.dot_general(kz, v, TN_DIMS, preferred_element_type=F32)

    mu = jnp.mean(o, axis=-1, keepdims=True)
    d = o - mu
    var = jnp.mean(d * d, axis=-1, keepdims=True)
    y = d * lax.rsqrt(var + EPS) * gn_ref[...]
    g = g_ref[...]
    y_ref[...] = (g * jax.nn.sigmoid(g) * y).astype(y_ref.dtype)


def retention_core(qk, v, g, gn, batch, seq):
    c = RET_BLOCK
    nc = seq // c
    dmask, xi, zeta, cdecay, cos, sin = _retention_tables(seq)
    row = lambda b, h, n: (b * nc + n, h)
    per_head = lambda b, h, n: (h, 0, 0)
    return pl.pallas_call(
        _retention_kernel,
        grid=(batch, RET_HEADS, nc),
        in_specs=[
            pl.BlockSpec((c, RET_QK_DIM), row),
            pl.BlockSpec((c, RET_QK_DIM), lambda b, h, n: (b * nc + n, RET_HEADS + h)),
            pl.BlockSpec((c, RET_V_DIM), row),
            pl.BlockSpec((c, RET_V_DIM), row),
            pl.BlockSpec((c, RET_QK_DIM // 2), lambda b, h, n: (n, 0)),
            pl.BlockSpec((c, RET_QK_DIM // 2), lambda b, h, n: (n, 0)),
            pl.BlockSpec((None, c, c), per_head),
            pl.BlockSpec((None, c, V7X_LANES), per_head),
            pl.BlockSpec((None, c, V7X_LANES), per_head),
            pl.BlockSpec((None, 1, RET_V_DIM), per_head),
            pl.BlockSpec((1, RET_V_DIM), lambda b, h, n: (0, h)),
        ],
        out_specs=pl.BlockSpec((c, RET_V_DIM), row),
        out_shape=jax.ShapeDtypeStruct(v.shape, BF16),
        scratch_shapes=[pltpu.VMEM((RET_QK_DIM, RET_V_DIM), F32)],
        compiler_params=_params("parallel", "parallel", "arbitrary"),
        name="retention_core",
    )(qk, qk, v, g, cos, sin, dmask, xi, zeta, cdecay, gn.reshape(1, -1))


def _attention_bias(rel_bias):
    i = jnp.arange(ATT_QBLOCK)[:, None]
    m = jnp.arange(ATT_WINDOW)[None, :]
    period = ATT_QBLOCK + ATT_WINDOW - 1
    lag = jnp.arange(period)
    lag = jnp.where(lag < ATT_WINDOW, lag, lag - period)
    rel = jnp.clip(LEFT_CHUNKS * REF_CHUNK - lag, -REL_CLIP, REL_CLIP) + REL_CLIP
    first = rel_bias.astype(F32)[:, rel]
    heads = rel_bias.shape[0]
    skew = jnp.tile(first, (1, ATT_QBLOCK + 1))[:, :ATT_QBLOCK * (period - 1)]
    toeplitz = skew.reshape(heads, ATT_QBLOCK, period - 1)[:, :, :ATT_WINDOW]
    ci = i // REF_CHUNK
    cm = m // REF_CHUNK
    band = (cm >= ci) & (cm <= ci + LEFT_CHUNKS)
    return jnp.where(band[None], toeplitz, MASK_VALUE)


def _attention_kernel(q_ref, k0_ref, k1_ref, k2_ref, v0_ref, v1_ref, v2_ref, bias_ref, o_ref):
    qblk = pl.program_id(1)
    n_prev = LEFT_CHUNKS * REF_CHUNK // ATT_QBLOCK
    col = lax.broadcasted_iota(jnp.int32, (1, ATT_WINDOW), 1)
    start_mask = jnp.where(col >= (n_prev - qblk) * ATT_QBLOCK, 0.0, MASK_VALUE)
    k_refs = (k0_ref, k1_ref, k2_ref)
    v_refs = (v0_ref, v1_ref, v2_ref)

    def one_head(h, carry):
        cols = pl.ds(pl.multiple_of(h * ATT_HEAD_DIM, ATT_HEAD_DIM), ATT_HEAD_DIM)
        q = q_ref[:, cols]
        s = jnp.concatenate(
            [lax.dot_general(q, kr[:, cols], NT_DIMS, preferred_element_type=F32) for kr in k_refs], axis=-1)
        s = s + bias_ref[h] + start_mask
        m = jnp.max(s, axis=-1, keepdims=True)
        p = jnp.exp(s - m)
        p = (p / jnp.sum(p, axis=-1, keepdims=True)).astype(BF16)
        o = jnp.dot(p[:, :ATT_QBLOCK], v_refs[0][:, cols], preferred_element_type=F32)
        for w in range(1, len(v_refs)):
            o = o + jnp.dot(p[:, w * ATT_QBLOCK:(w + 1) * ATT_QBLOCK], v_refs[w][:, cols],
                            preferred_element_type=F32)
        o_ref[:, cols] = o.astype(o_ref.dtype)
        return carry

    lax.fori_loop(0, ATT_HEADS, one_head, 0, unroll=ATT_HEAD_UNROLL)


def attention_core(q, kv, bias, batch, seq):
    t, d = q.shape
    nq = seq // ATT_QBLOCK
    n_prev = LEFT_CHUNKS * REF_CHUNK // ATT_QBLOCK
    assert ATT_WINDOW == (n_prev + 1) * ATT_QBLOCK

    def window(w, part):
        return pl.BlockSpec((ATT_QBLOCK, d), lambda b, n: (b * nq + jnp.maximum(n - n_prev + w, 0), part))

    return pl.pallas_call(
        _attention_kernel,
        grid=(batch, nq),
        in_specs=[pl.BlockSpec((ATT_QBLOCK, d), lambda b, n: (b * nq + n, 0))]
        + [window(w, 0) for w in range(n_prev + 1)]
        + [window(w, 1) for w in range(n_prev + 1)]
        + [pl.BlockSpec(bias.shape, lambda b, n: (0, 0, 0))],
        out_specs=pl.BlockSpec((ATT_QBLOCK, d), lambda b, n: (b * nq + n, 0)),
        out_shape=jax.ShapeDtypeStruct((t, d), BF16),
        compiler_params=_params("parallel", "arbitrary"),
        name="attention_core",
    )(q, kv, kv, kv, kv, kv, kv, bias)


PEER_RANKS = PEER_TOPK + 1
V7X_SUBLANES = 8


def _sorting_network(n):
    pairs = []
    p = 1
    while p < n:
        k = p
        while k >= 1:
            for j in range(k % p, n - k, 2 * k):
                for i in range(min(k, n - j - k)):
                    if (i + j) // (2 * p) == (i + j + k) // (2 * p):
                        pairs.append((i + j, i + j + k))
            k //= 2
        p *= 2
    return pairs


def _sort_descending(vals):
    vals = list(vals)
    for i, j in _sorting_network(len(vals)):
        vals[i], vals[j] = jnp.maximum(vals[i], vals[j]), jnp.minimum(vals[i], vals[j])
    return vals


def _top_ranked(s):
    n = PEER_TOPK
    tiles = [s[r * V7X_SUBLANES:(r + 1) * V7X_SUBLANES, :] for r in range(s.shape[0] // V7X_SUBLANES)]
    assert len(tiles) == n
    top = _sort_descending(tiles)
    shift = V7X_SUBLANES // 2
    while shift >= 1:
        other = [pltpu.roll(v, shift, 0) for v in top]
        top = [jnp.maximum(top[i], other[n - 1 - i]) for i in range(n)]
        k = n // 2
        while k >= 1:
            for i in range(n):
                if i & k == 0:
                    top[i], top[i + k] = jnp.maximum(top[i], top[i + k]), jnp.minimum(top[i], top[i + k])
            k //= 2
        shift //= 2
    below = [jnp.where(t < top[n - 1], t, NEG_INF) for t in tiles]
    while len(below) > 1:
        below = [jnp.maximum(below[2 * i], below[2 * i + 1]) for i in range(len(below) // 2)]
    nxt = below[0]
    shift = V7X_SUBLANES // 2
    while shift >= 1:
        nxt = jnp.maximum(nxt, pltpu.roll(nxt, shift, 0))
        shift //= 2
    return top + [nxt]


def _bf16_twice(x):
    hi = pltpu.bitcast(x.astype(BF16).astype(F32), jnp.uint32)
    return hi | (hi >> 16)


def _peer_prep_kernel(x_ref, g_ref, wqt_ref, sk_ref, hnt_ref, b2_ref, e2_ref, n1_ref, c_ref, qt_ref):
    tm = x_ref.shape[0]
    lane_groups = tm // V7X_LANES
    assert lane_groups <= V7X_SUBLANES
    hnt = _rmsnorm_rows(x_ref[...], g_ref[...]).T.astype(BF16)
    hnt_ref[...] = hnt
    qt_ref[...] = jnp.dot(wqt_ref[...], hnt, preferred_element_type=F32)
    sublane = lax.broadcasted_iota(jnp.int32, (V7X_SUBLANES, V7X_LANES), 0)

    def one_head(h, carry):
        scores = []
        ranked = []
        for part in range(2):
            rows = pl.ds(pl.multiple_of(h * (2 * PEER_HALF) + part * PEER_HALF, PEER_HALF), PEER_HALF)
            s = jnp.dot(sk_ref[part], qt_ref[rows, :].astype(BF16), preferred_element_type=F32)
            scores.append(s)
            per_group = [_top_ranked(s[:, g * V7X_LANES:(g + 1) * V7X_LANES]) for g in range(lane_groups)]
            dense = []
            for r in range(PEER_RANKS):
                tile = per_group[lane_groups - 1][r]
                for g in range(lane_groups - 2, -1, -1):
                    tile = jnp.where(sublane == g, per_group[g][r], tile)
                dense.append(tile)
            ranked.append(dense)

        xs, ys = ranked
        sums = [xs[i] + ys[j] for i in range(PEER_RANKS) for j in range(PEER_RANKS) if (i + 1) * (j + 1) <= PEER_RANKS]
        size = 1
        while size < len(sums):
            size *= 2
        sums = _sort_descending(sums + [jnp.full_like(sums[0], NEG_INF)] * (size - len(sums)))
        z = jnp.ones_like(sums[0])
        for v in sums[1:PEER_TOPK]:
            z = z + jnp.exp(v - sums[0])
        tau = 0.5 * (sums[PEER_TOPK - 1] + sums[PEER_TOPK])
        half_over_z = 0.5 / z

        s1, s2 = scores
        for g in range(lane_groups):
            lanes = slice(g * V7X_LANES, (g + 1) * V7X_LANES)
            row = lambda t, g=g: t[g:g + 1, :]
            s1_g = s1[:, lanes]
            s2_g = s2[:, lanes]
            thr = row(tau) - s1_g
            b2 = jnp.zeros_like(s2_g)
            n1 = jnp.zeros_like(s1_g)
            for r in range(PEER_TOPK):
                y = row(ys[r])
                b2 = b2 + jnp.where(y > s2_g, 1.0, 0.0)
                n1 = n1 + jnp.where(y >= thr, 1.0, 0.0)
            c = jnp.exp(s1_g - row(xs[0])) * row(half_over_z)
            n1_ref[h, :, lanes] = _bf16_twice(n1)
            c_ref[h, :, lanes] = _bf16_twice(c)
            b2_ref[h, g] = pltpu.bitcast(b2.astype(BF16), jnp.uint32)
            e2_ref[h, g] = pltpu.bitcast(jnp.exp(s2_g - row(ys[0])).astype(BF16), jnp.uint32)
        return carry

    lax.fori_loop(0, PEER_HEADS, one_head, 0)


def peer_prep(x, gain, wq_t, sub_keys, *, tm=512):
    t, d = x.shape
    tm = min(tm, t)
    assert t % tm == 0 and tm % V7X_LANES == 0
    nq = wq_t.shape[0]
    fac = jax.ShapeDtypeStruct((PEER_HEADS, PEER_NKEYS, t), jnp.uint32)
    fac_spec = pl.BlockSpec((PEER_HEADS, PEER_NKEYS, tm), lambda i: (0, 0, i))
    blocked = jax.ShapeDtypeStruct((PEER_HEADS, t // V7X_LANES, PEER_NKEYS // 2, V7X_LANES), jnp.uint32)
    blocked_spec = pl.BlockSpec((PEER_HEADS, tm // V7X_LANES, PEER_NKEYS // 2, V7X_LANES), lambda i: (0, i, 0, 0))
    return pl.pallas_call(
        _peer_prep_kernel,
        grid=(t // tm,),
        in_specs=[
            pl.BlockSpec((tm, d), lambda i: (i, 0)),
            pl.BlockSpec((1, d), lambda i: (0, 0)),
            pl.BlockSpec((nq, d), lambda i: (0, 0)),
            pl.BlockSpec(sub_keys.shape, lambda i: (0, 0, 0)),
        ],
        out_specs=[pl.BlockSpec((d, tm), lambda i: (0, i)), blocked_spec, blocked_spec, fac_spec, fac_spec],
        out_shape=[jax.ShapeDtypeStruct((d, t), BF16), blocked, blocked, fac, fac],
        scratch_shapes=[pltpu.VMEM((nq, tm), F32)],
        compiler_params=_params("parallel"),
        name="peer_prep",
    )(x, gain.reshape(1, d), wq_t, sub_keys)


PEER_KEYS_PER_STEP = 8
PEER_ETILE = PEER_KEYS_PER_STEP * PEER_NKEYS
PEER_SUBROWS = 16


def _peer_main_kernel(hnt_first_ref, u_first_ref, hnt_ref, u_ref, vt_ref, b2_ref, e2_ref, n1_ref, c_ref, x_ref,
                      o_ref, acc_ref, at_ref, p_ref, *, tiles_per_block):
    g = pl.program_id(0)
    j = lax.rem(g, tiles_per_block)
    tm = hnt_ref.shape[1]
    lane_groups = tm // V7X_LANES

    halves = 2
    keys_per_half = PEER_KEYS_PER_STEP // halves
    rows_per_half = PEER_ETILE // halves
    words = PEER_SUBROWS // 2
    word_tile = (words, V7X_LANES)

    def packed(x):
        return pltpu.bitcast(x, BF16)

    def pre_activation(u_tile_ref, hn_ref, slot):
        at_ref[slot] = jnp.dot(u_tile_ref[...], hn_ref[...], preferred_element_type=F32)

    @pl.when(g == 0)
    def _():
        pre_activation(u_first_ref, hnt_first_ref, 0)

    @pl.when(j == 0)
    def _():
        acc_ref[...] = jnp.zeros_like(acc_ref)

    def weighted_gelu(e, slot):
        for ka in range(e * keys_per_half, (e + 1) * keys_per_half):
            for lg in range(lane_groups):
                lanes = slice(lg * V7X_LANES, (lg + 1) * V7X_LANES)
                n1 = [packed(jnp.broadcast_to(n1_ref[h, ka:ka + 1, lanes], word_tile)) for h in range(PEER_HEADS)]
                cw = [packed(jnp.broadcast_to(c_ref[h, ka:ka + 1, lanes], word_tile)) for h in range(PEER_HEADS)]
                for bg in range(PEER_NKEYS // PEER_SUBROWS):
                    wrows = slice(bg * words, (bg + 1) * words)
                    w = None
                    for h in range(PEER_HEADS):
                        term = jnp.where(packed(b2_ref[h, lg, wrows, :]) < n1[h],
                                         packed(e2_ref[h, lg, wrows, :]) * cw[h], 0.0)
                        w = term if w is None else w + term
                    first = ka * PEER_NKEYS + bg * PEER_SUBROWS
                    pre = at_ref[slot, first:first + PEER_SUBROWS, lanes]
                    act = pre * (1.0 + lax.erf(pre * (2.0 ** -0.5)))
                    p_ref[lg, first // 2:first // 2 + words, :] = pltpu.bitcast(act.astype(BF16) * w, jnp.uint32)

    def value_matmul(e):
        rows = slice(e * rows_per_half, (e + 1) * rows_per_half)
        wrows = slice(e * rows_per_half // 2, (e + 1) * rows_per_half // 2)
        p_half = jnp.concatenate([packed(p_ref[lg, wrows, :]) for lg in range(lane_groups)], axis=-1)
        acc_ref[...] += jnp.dot(vt_ref[:, rows], p_half, preferred_element_type=F32)

    def step(slot):
        pre_activation(u_ref, hnt_ref, 1 - slot)
        weighted_gelu(0, slot)
        value_matmul(0)
        weighted_gelu(1, slot)
        value_matmul(1)

    for parity in range(2):
        pl.when(lax.rem(g, 2) == parity)(functools.partial(step, parity))

    @pl.when(j == tiles_per_block - 1)
    def _():
        o_ref[...] = x_ref[...] + acc_ref[...].T


def peer_main(hn_t, u, v_t, b2, e2, n1, c, x, *, tm=512):
    t, d = x.shape
    n_exp = u.shape[0]
    assert v_t.shape == (n_exp // PEER_ETILE, d, PEER_ETILE)
    tm = min(tm, t)
    assert t % tm == 0 and tm % V7X_LANES == 0
    assert n_exp % PEER_ETILE == 0 and n_exp == PEER_NKEYS * PEER_NKEYS
    lane_groups = tm // V7X_LANES
    n_i = t // tm
    n_j = n_exp // PEER_ETILE
    assert n_j % 2 == 0

    def block(g):
        return g // n_j

    def tile(g):
        return lax.rem(g, n_j)

    def nxt(g):
        return jnp.minimum(g + 1, n_i * n_j - 1)

    all_keys = pl.BlockSpec((PEER_HEADS, lane_groups, PEER_NKEYS // 2, V7X_LANES), lambda g: (0, block(g), 0, 0))
    step_keys = pl.BlockSpec((PEER_HEADS, PEER_KEYS_PER_STEP, tm), lambda g: (0, tile(g), block(g)))
    return pl.pallas_call(
        functools.partial(_peer_main_kernel, tiles_per_block=n_j),
        grid=(n_i * n_j,),
        in_specs=[
            pl.BlockSpec((d, tm), lambda g: (0, 0), pipeline_mode=pl.Buffered(1)),
            pl.BlockSpec((PEER_ETILE, d), lambda g: (0, 0), pipeline_mode=pl.Buffered(1)),
            pl.BlockSpec((d, tm), lambda g: (0, block(nxt(g)))),
            pl.BlockSpec((PEER_ETILE, d), lambda g: (tile(nxt(g)), 0)),
            pl.BlockSpec((None, d, PEER_ETILE), lambda g: (tile(g), 0, 0)),
            all_keys, all_keys, step_keys, step_keys,
            pl.BlockSpec((tm, d), lambda g: (block(g), 0)),
        ],
        out_specs=pl.BlockSpec((tm, d), lambda g: (block(g), 0)),
        out_shape=jax.ShapeDtypeStruct((t, d), F32),
        scratch_shapes=[
            pltpu.VMEM((d, tm), F32),
            pltpu.VMEM((2, PEER_ETILE, tm), F32),
            pltpu.VMEM((lane_groups, PEER_ETILE // 2, V7X_LANES), jnp.uint32),
        ],
        compiler_params=_params("arbitrary"),
        name="peer_main",
    )(hn_t, u, hn_t, u, v_t, b2, e2, n1, c, x)


def peer_layer(x, gain, wq_t, sub_keys, u, v_t):
    hn_t, b2, e2, n1, c = peer_prep(x, gain, wq_t, sub_keys)
    return peer_main(hn_t, u, v_t, b2, e2, n1, c, x)


def _final_norm_kernel(x_ref, g_ref, o_ref):
    o_ref[...] = _rmsnorm_rows(x_ref[...], g_ref[...])


def final_norm(x, gain, *, tm=1024):
    t, d = x.shape
    tm = min(tm, t)
    return pl.pallas_call(
        _final_norm_kernel,
        grid=(t // tm,),
        in_specs=[pl.BlockSpec((tm, d), lambda i: (i, 0)), pl.BlockSpec((1, d), lambda i: (0, 0))],
        out_specs=pl.BlockSpec((tm, d), lambda i: (i, 0)),
        out_shape=jax.ShapeDtypeStruct((t, d), F32),
        compiler_params=_params("parallel"),
        name="final_norm",
    )(x, gain.reshape(1, d))


def kernel(x, ln_mix, ln_ffn, ret_w_in, ret_w_out, ret_gn, kv_norm, w_kv, att_w_q, att_w_o, att_rel_bias,
           peer_w_q, peer_sub_keys, peer_u, peer_v, ln_final):
    batch, seq, d = x.shape
    depth = ln_mix.shape[0]
    n_a = ret_w_in.shape[0]
    xt = x.reshape(batch * seq, d)
    qk_width = 2 * RET_HEADS * RET_QK_DIM
    v_width = RET_HEADS * RET_V_DIM
    kv = None
    w_in_all = ret_w_in.astype(BF16)
    for l in range(depth):
        if l < n_a:
            qk, v, g = norm_matmul_split(xt, ln_mix[l], w_in_all, l, (qk_width, v_width, v_width), (F32, BF16, F32))
            y = retention_core(qk, v, g, ret_gn[l], batch, seq)
            xt = matmul_residual(y, ret_w_out[l].astype(BF16), xt)
        else:
            j = l - n_a
            q = norm_matmul(xt, ln_mix[l], att_w_q[j].astype(BF16), col_start=0, n_cols=d, out_dtype=BF16,
                            scale=ATT_HEAD_DIM ** -0.5)
            a = attention_core(q, kv, _attention_bias(att_rel_bias[j]), batch, seq)
            xt = matmul_residual(a, att_w_o[j].astype(BF16), xt)
        xt = peer_layer(xt, ln_ffn[l], peer_w_q[l].T.astype(BF16), peer_sub_keys[l].astype(BF16),
                        peer_u[l].astype(BF16),
                        peer_v[l].reshape(-1, PEER_ETILE, d).transpose(0, 2, 1).astype(BF16))
        if l == n_a - 1:
            kv = norm_matmul(xt, kv_norm, w_kv.astype(BF16), col_start=0, n_cols=2 * d, out_dtype=BF16)
    return final_norm(xt, ln_final).reshape(batch, seq, d)
```

```python
import functools

import jax
import jax.numpy as jnp
from jax import lax
from jax.experimental import pallas as pl
from jax.experimental.pallas import tpu as pltpu

F32 = jnp.float32
BF16 = jnp.bfloat16

EPS = 1e-6
ROPE_BASE = 10000.0
REF_CHUNK = 64
RET_HEADS = 8
RET_QK_DIM = 256
RET_V_DIM = 512
RET_BLOCK = 512
ATT_HEADS = 16
ATT_HEAD_DIM = 128
LEFT_CHUNKS = 8
REL_CLIP = 128
ATT_QBLOCK = 256
ATT_WINDOW = ATT_QBLOCK + LEFT_CHUNKS * REF_CHUNK
ATT_HEAD_UNROLL = 4
PEER_HEADS = 8
PEER_NKEYS = 128
PEER_HALF = 128
PEER_TOPK = 16
MASK_VALUE = -1e30
NEG_INF = float("-inf")

V7X_LANES = 128
V7X_VMEM_LIMIT_BYTES = 60 * 1024 * 1024

NT_DIMS = (((1,), (1,)), ((), ()))
TN_DIMS = (((0,), (0,)), ((), ()))


def _params(*sem, flags=None):
    return pltpu.CompilerParams(dimension_semantics=sem, vmem_limit_bytes=V7X_VMEM_LIMIT_BYTES, flags=flags)


def _rmsnorm_rows(x, g):
    ms = jnp.mean(x * x, axis=-1, keepdims=True)
    return x * lax.rsqrt(ms + EPS) * g


def _norm_matmul_kernel(x_ref, g_ref, w_ref, o_ref, hn_ref, *, scale):
    @pl.when(pl.program_id(1) == 0)
    def _():
        hn_ref[...] = _rmsnorm_rows(x_ref[...], g_ref[...]).astype(BF16)

    acc = jnp.dot(hn_ref[...], w_ref[...], preferred_element_type=F32)
    if scale != 1.0:
        acc = acc * scale
    o_ref[...] = acc.astype(o_ref.dtype)


def norm_matmul(x, gain, w, *, col_start, n_cols, out_dtype, scale=1.0, tm=1024, tn=1024):
    t, d = x.shape
    tm = min(tm, t)
    tn = min(tn, n_cols)
    assert t % tm == 0 and n_cols % tn == 0 and col_start % tn == 0
    off = col_start // tn
    return pl.pallas_call(
        functools.partial(_norm_matmul_kernel, scale=scale),
        grid=(t // tm, n_cols // tn),
        in_specs=[
            pl.BlockSpec((tm, d), lambda i, j: (i, 0)),
            pl.BlockSpec((1, d), lambda i, j: (0, 0)),
            pl.BlockSpec((d, tn), lambda i, j: (0, j + off)),
        ],
        out_specs=pl.BlockSpec((tm, tn), lambda i, j: (i, j)),
        out_shape=jax.ShapeDtypeStruct((t, n_cols), out_dtype),
        scratch_shapes=[pltpu.VMEM((tm, d), BF16)],
        compiler_params=_params("parallel", "arbitrary"),
        name="norm_matmul",
    )(x, gain.reshape(1, d), w)


def _norm_matmul_split_kernel(x_ref, g_ref, w_ref, *refs, tile_bounds):
    out_refs, hn_ref = refs[:-1], refs[-1]
    j = pl.program_id(1)

    @pl.when(j == 0)
    def _():
        hn_ref[...] = _rmsnorm_rows(x_ref[...], g_ref[...]).astype(BF16)

    for o_ref, (lo, hi) in zip(out_refs, tile_bounds):
        @pl.when((j >= lo) & (j < hi))
        def _(o_ref=o_ref):
            o_ref[...] = jnp.dot(hn_ref[...], w_ref[...], preferred_element_type=F32).astype(o_ref.dtype)


def norm_matmul_split(x, gain, w, layer, widths, dtypes, *, tm=1024, tn=1024):
    t, d = x.shape
    tm = min(tm, t)
    assert t % tm == 0 and all(width % tn == 0 for width in widths) and sum(widths) == w.shape[-1]
    tile_bounds = []
    lo = 0
    for width in widths:
        tile_bounds.append((lo, lo + width // tn))
        lo += width // tn

    def out_spec(lo, hi):
        return pl.BlockSpec((tm, tn), lambda i, j: (i, jnp.clip(j - lo, 0, hi - lo - 1)))

    return pl.pallas_call(
        functools.partial(_norm_matmul_split_kernel, tile_bounds=tuple(tile_bounds)),
        grid=(t // tm, lo),
        in_specs=[
            pl.BlockSpec((tm, d), lambda i, j: (i, 0)),
            pl.BlockSpec((1, d), lambda i, j: (0, 0)),
            pl.BlockSpec((None, d, tn), lambda i, j: (layer, 0, j)),
        ],
        out_specs=[out_spec(*b) for b in tile_bounds],
        out_shape=[jax.ShapeDtypeStruct((t, width), dt) for width, dt in zip(widths, dtypes)],
        scratch_shapes=[pltpu.VMEM((tm, d), BF16)],
        compiler_params=_params("parallel", "arbitrary"),
        name="norm_matmul_split",
    )(x, gain.reshape(1, d), w)


def _matmul_residual_kernel(y_ref, w_ref, x_ref, o_ref):
    o_ref[...] = x_ref[...] + jnp.dot(y_ref[...], w_ref[...], preferred_element_type=F32)


MATMUL_WEIGHT_TILE_BYTES = 4 * 1024 * 1024


def matmul_residual(y, w, x, *, tm=1024):
    t, k = y.shape
    n = w.shape[1]
    tm = min(tm, t)
    tn = min(MATMUL_WEIGHT_TILE_BYTES // (2 * k), n)
    assert t % tm == 0 and n % tn == 0
    return pl.pallas_call(
        _matmul_residual_kernel,
        grid=(t // tm, n // tn),
        in_specs=[
            pl.BlockSpec((tm, k), lambda i, j: (i, 0)),
            pl.BlockSpec((k, tn), lambda i, j: (0, j)),
            pl.BlockSpec((tm, tn), lambda i, j: (i, j)),
        ],
        out_specs=pl.BlockSpec((tm, tn), lambda i, j: (i, j)),
        out_shape=jax.ShapeDtypeStruct((t, n), F32),
        compiler_params=_params("parallel", "arbitrary"),
        name="matmul_residual",
    )(y, w, x)


def _retention_tables(seq):
    c = RET_BLOCK
    log_gamma = jnp.log1p(-jnp.exp2(-5.0 - jnp.arange(RET_HEADS, dtype=F32)))
    idx = jnp.arange(c, dtype=F32)
    diff = idx[:, None] - idx[None, :]
    dmask = jnp.where(diff >= 0, jnp.exp(log_gamma[:, None, None] * jnp.maximum(diff, 0.0)), 0.0)
    xi = jnp.exp(log_gamma[:, None] * (idx[None, :] + 1.0))
    zeta = jnp.exp(log_gamma[:, None] * (c - 1.0 - idx[None, :]))
    cdecay = jnp.exp(log_gamma * c)
    xi = jnp.broadcast_to(xi[:, :, None], (RET_HEADS, c, V7X_LANES))
    zeta = jnp.broadcast_to(zeta[:, :, None], (RET_HEADS, c, V7X_LANES))
    cdecay = jnp.broadcast_to(cdecay[:, None, None], (RET_HEADS, 1, RET_V_DIM))
    half = RET_QK_DIM // 2
    inv = 1.0 / (ROPE_BASE ** (jnp.arange(half, dtype=F32) / half))
    ang = jnp.arange(seq, dtype=F32)[:, None] * inv[None, :]
    return dmask, xi, zeta, cdecay, jnp.cos(ang), jnp.sin(ang)


def _retention_kernel(q_ref, k_ref, v_ref, g_ref, cos_ref, sin_ref, dmask_ref, xi_ref, zeta_ref, cd_ref,
                      gn_ref, y_ref, state_ref):
    @pl.when(pl.program_id(2) == 0)
    def _():
        state_ref[...] = jnp.zeros_like(state_ref)

    cos = cos_ref[...]
    sin = sin_ref[...]
    half = RET_QK_DIM // 2

    def rotate(t):
        t1 = t[:, :half]
        t2 = t[:, half:]
        return jnp.concatenate([t1 * cos - t2 * sin, t1 * sin + t2 * cos], axis=-1)

    q = rotate(q_ref[...])
    k = rotate(k_ref[...]) * (RET_QK_DIM ** -0.5)
    qb = q.astype(BF16)
    kb = k.astype(BF16)
    v = v_ref[...]

    scores = lax.dot_general(qb, kb, NT_DIMS, preferred_element_type=F32) * dmask_ref[...]
    intra = jnp.dot(scores.astype(BF16), v, preferred_element_type=F32)
    state = state_ref[...]
    xi = jnp.concatenate([xi_ref[...]] * (RET_V_DIM // V7X_LANES), axis=-1)
    cross = jnp.dot(qb, state.astype(BF16), preferred_element_type=F32) * xi
    o = intra + cross

    zeta = jnp.concatenate([zeta_ref[...]] * (RET_QK_DIM // V7X_LANES), axis=-1)
    kz = (k * zeta).astype(BF16)
    state_ref[...] = state * cd_ref[...] + lax.dot_general(kz, v, TN_DIMS, preferred_element_type=F32)

    mu = jnp.mean(o, axis=-1, keepdims=True)
    d = o - mu
    var = jnp.mean(d * d, axis=-1, keepdims=True)
    y = d * lax.rsqrt(var + EPS) * gn_ref[...]
    g = g_ref[...]
    y_ref[...] = (g * jax.nn.sigmoid(g) * y).astype(y_ref.dtype)


def retention_core(qk, v, g, gn, batch, seq):
    c = RET_BLOCK
    nc = seq // c
    dmask, xi, zeta, cdecay, cos, sin = _retention_tables(seq)
    row = lambda b, h, n: (b * nc + n, h)
    per_head = lambda b, h, n: (h, 0, 0)
    return pl.pallas_call(
        _retention_kernel,
        grid=(batch, RET_HEADS, nc),
        in_specs=[
            pl.BlockSpec((c, RET_QK_DIM), row),
            pl.BlockSpec((c, RET_QK_DIM), lambda b, h, n: (b * nc + n, RET_HEADS + h)),
            pl.BlockSpec((c, RET_V_DIM), row),
            pl.BlockSpec((c, RET_V_DIM), row),
            pl.BlockSpec((c, RET_QK_DIM // 2), lambda b, h, n: (n, 0)),
            pl.BlockSpec((c, RET_QK_DIM // 2), lambda b, h, n: (n, 0)),
            pl.BlockSpec((None, c, c), per_head),
            pl.BlockSpec((None, c, V7X_LANES), per_head),
            pl.BlockSpec((None, c, V7X_LANES), per_head),
            pl.BlockSpec((None, 1, RET_V_DIM), per_head),
            pl.BlockSpec((1, RET_V_DIM), lambda b, h, n: (0, h)),
        ],
        out_specs=pl.BlockSpec((c, RET_V_DIM), row),
        out_shape=jax.ShapeDtypeStruct(v.shape, BF16),
        scratch_shapes=[pltpu.VMEM((RET_QK_DIM, RET_V_DIM), F32)],
        compiler_params=_params("parallel", "parallel", "arbitrary"),
        name="retention_core",
    )(qk, qk, v, g, cos, sin, dmask, xi, zeta, cdecay, gn.reshape(1, -1))


def _attention_bias(rel_bias):
    i = jnp.arange(ATT_QBLOCK)[:, None]
    m = jnp.arange(ATT_WINDOW)[None, :]
    period = ATT_QBLOCK + ATT_WINDOW - 1
    lag = jnp.arange(period)
    lag = jnp.where(lag < ATT_WINDOW, lag, lag - period)
    rel = jnp.clip(LEFT_CHUNKS * REF_CHUNK - lag, -REL_CLIP, REL_CLIP) + REL_CLIP
    first = rel_bias.astype(F32)[:, rel]
    heads = rel_bias.shape[0]
    skew = jnp.tile(first, (1, ATT_QBLOCK + 1))[:, :ATT_QBLOCK * (period - 1)]
    toeplitz = skew.reshape(heads, ATT_QBLOCK, period - 1)[:, :, :ATT_WINDOW]
    ci = i // REF_CHUNK
    cm = m // REF_CHUNK
    band = (cm >= ci) & (cm <= ci + LEFT_CHUNKS)
    return jnp.where(band[None], toeplitz, MASK_VALUE)


def _attention_kernel(q_ref, k0_ref, k1_ref, k2_ref, v0_ref, v1_ref, v2_ref, bias_ref, o_ref):
    qblk = pl.program_id(1)
    n_prev = LEFT_CHUNKS * REF_CHUNK // ATT_QBLOCK
    col = lax.broadcasted_iota(jnp.int32, (1, ATT_WINDOW), 1)
    start_mask = jnp.where(col >= (n_prev - qblk) * ATT_QBLOCK, 0.0, MASK_VALUE)
    k_refs = (k0_ref, k1_ref, k2_ref)
    v_refs = (v0_ref, v1_ref, v2_ref)

    def one_head(h, carry):
        cols = pl.ds(pl.multiple_of(h * ATT_HEAD_DIM, ATT_HEAD_DIM), ATT_HEAD_DIM)
        q = q_ref[:, cols]
        s = jnp.concatenate(
            [lax.dot_general(q, kr[:, cols], NT_DIMS, preferred_element_type=F32) for kr in k_refs], axis=-1)
        s = s + bias_ref[h] + start_mask
        m = jnp.max(s, axis=-1, keepdims=True)
        p = jnp.exp(s - m)
        p = (p / jnp.sum(p, axis=-1, keepdims=True)).astype(BF16)
        o = jnp.dot(p[:, :ATT_QBLOCK], v_refs[0][:, cols], preferred_element_type=F32)
        for w in range(1, len(v_refs)):
            o = o + jnp.dot(p[:, w * ATT_QBLOCK:(w + 1) * ATT_QBLOCK], v_refs[w][:, cols],
                            preferred_element_type=F32)
        o_ref[:, cols] = o.astype(o_ref.dtype)
        return carry

    lax.fori_loop(0, ATT_HEADS, one_head, 0, unroll=ATT_HEAD_UNROLL)


def attention_core(q, kv, bias, batch, seq):
    t, d = q.shape
    nq = seq // ATT_QBLOCK
    n_prev = LEFT_CHUNKS * REF_CHUNK // ATT_QBLOCK
    assert ATT_WINDOW == (n_prev + 1) * ATT_QBLOCK

    def window(w, part):
        return pl.BlockSpec((ATT_QBLOCK, d), lambda b, n: (b * nq + jnp.maximum(n - n_prev + w, 0), part))

    return pl.pallas_call(
        _attention_kernel,
        grid=(batch, nq),
        in_specs=[pl.BlockSpec((ATT_QBLOCK, d), lambda b, n: (b * nq + n, 0))]
        + [window(w, 0) for w in range(n_prev + 1)]
        + [window(w, 1) for w in range(n_prev + 1)]
        + [pl.BlockSpec(bias.shape, lambda b, n: (0, 0, 0))],
        out_specs=pl.BlockSpec((ATT_QBLOCK, d), lambda b, n: (b * nq + n, 0)),
        out_shape=jax.ShapeDtypeStruct((t, d), BF16),
        compiler_params=_params("parallel", "arbitrary"),
        name="attention_core",
    )(q, kv, kv, kv, kv, kv, kv, bias)


PEER_RANKS = PEER_TOPK + 1
V7X_SUBLANES = 8


def _sorting_network(n):
    pairs = []
    p = 1
    while p < n:
        k = p
        while k >= 1:
            for j in range(k % p, n - k, 2 * k):
                for i in range(min(k, n - j - k)):
                    if (i + j) // (2 * p) == (i + j + k) // (2 * p):
                        pairs.append((i + j, i + j + k))
            k //= 2
        p *= 2
    return pairs


def _sort_descending(vals):
    vals = list(vals)
    for i, j in _sorting_network(len(vals)):
        vals[i], vals[j] = jnp.maximum(vals[i], vals[j]), jnp.minimum(vals[i], vals[j])
    return vals


def _top_ranked(s):
    n = PEER_TOPK
    tiles = [s[r * V7X_SUBLANES:(r + 1) * V7X_SUBLANES, :] for r in range(s.shape[0] // V7X_SUBLANES)]
    assert len(tiles) == n
    top = _sort_descending(tiles)
    shift = V7X_SUBLANES // 2
    while shift >= 1:
        other = [pltpu.roll(v, shift, 0) for v in top]
        top = [jnp.maximum(top[i], other[n - 1 - i]) for i in range(n)]
        k = n // 2
        while k >= 1:
            for i in range(n):
                if i & k == 0:
                    top[i], top[i + k] = jnp.maximum(top[i], top[i + k]), jnp.minimum(top[i], top[i + k])
            k //= 2
        shift //= 2
    below = [jnp.where(t < top[n - 1], t, NEG_INF) for t in tiles]
    while len(below) > 1:
        below = [jnp.maximum(below[2 * i], below[2 * i + 1]) for i in range(len(below) // 2)]
    nxt = below[0]
    shift = V7X_SUBLANES // 2
    while shift >= 1:
        nxt = jnp.maximum(nxt, pltpu.roll(nxt, shift, 0))
        shift //= 2
    return top + [nxt]


def _bf16_twice(x):
    hi = pltpu.bitcast(x.astype(BF16).astype(F32), jnp.uint32)
    return hi | (hi >> 16)


def _peer_prep_kernel(x_ref, g_ref, wqt_ref, sk_ref, hnt_ref, b2_ref, e2_ref, n1_ref, c_ref, qt_ref):
    tm = x_ref.shape[0]
    lane_groups = tm // V7X_LANES
    assert lane_groups <= V7X_SUBLANES
    hnt = _rmsnorm_rows(x_ref[...], g_ref[...]).T.astype(BF16)
    hnt_ref[...] = hnt
    qt_ref[...] = jnp.dot(wqt_ref[...], hnt, preferred_element_type=F32)
    sublane = lax.broadcasted_iota(jnp.int32, (V7X_SUBLANES, V7X_LANES), 0)

    def one_head(h, carry):
        scores = []
        ranked = []
        for part in range(2):
            rows = pl.ds(pl.multiple_of(h * (2 * PEER_HALF) + part * PEER_HALF, PEER_HALF), PEER_HALF)
            s = jnp.dot(sk_ref[part], qt_ref[rows, :].astype(BF16), preferred_element_type=F32)
            scores.append(s)
            per_group = [_top_ranked(s[:, g * V7X_LANES:(g + 1) * V7X_LANES]) for g in range(lane_groups)]
            dense = []
            for r in range(PEER_RANKS):
                tile = per_group[lane_groups - 1][r]
                for g in range(lane_groups - 2, -1, -1):
                    tile = jnp.where(sublane == g, per_group[g][r], tile)
                dense.append(tile)
            ranked.append(dense)

        xs, ys = ranked
        sums = [xs[i] + ys[j] for i in range(PEER_RANKS) for j in range(PEER_RANKS) if (i + 1) * (j + 1) <= PEER_RANKS]
        size = 1
        while size < len(sums):
            size *= 2
        sums = _sort_descending(sums + [jnp.full_like(sums[0], NEG_INF)] * (size - len(sums)))
        z = jnp.ones_like(sums[0])
        for v in sums[1:PEER_TOPK]:
            z = z + jnp.exp(v - sums[0])
        tau = 0.5 * (sums[PEER_TOPK - 1] + sums[PEER_TOPK])
        half_over_z = 0.5 / z

        s1, s2 = scores
        for g in range(lane_groups):
            lanes = slice(g * V7X_LANES, (g + 1) * V7X_LANES)
            row = lambda t, g=g: t[g:g + 1, :]
            s1_g = s1[:, lanes]
            s2_g = s2[:, lanes]
            thr = row(tau) - s1_g
            b2 = jnp.zeros_like(s2_g)
            n1 = jnp.zeros_like(s1_g)
            for r in range(PEER_TOPK):
                y = row(ys[r])
                b2 = b2 + jnp.where(y > s2_g, 1.0, 0.0)
                n1 = n1 + jnp.where(y >= thr, 1.0, 0.0)
            c = jnp.exp(s1_g - row(xs[0])) * row(half_over_z)
            n1_ref[h, :, lanes] = _bf16_twice(n1)
            c_ref[h, :, lanes] = _bf16_twice(c)
            b2_ref[h, g] = pltpu.bitcast(b2.astype(BF16), jnp.uint32)
            e2_ref[h, g] = pltpu.bitcast(jnp.exp(s2_g - row(ys[0])).astype(BF16), jnp.uint32)
        return carry

    lax.fori_loop(0, PEER_HEADS, one_head, 0)


def peer_prep(x, gain, wq_t, sub_keys, *, tm=512):
    t, d = x.shape
    tm = min(tm, t)
    assert t % tm == 0 and tm % V7X_LANES == 0
    nq = wq_t.shape[0]
    fac = jax.ShapeDtypeStruct((PEER_HEADS, PEER_NKEYS, t), jnp.uint32)
    fac_spec = pl.BlockSpec((PEER_HEADS, PEER_NKEYS, tm), lambda i: (0, 0, i))
    blocked = jax.ShapeDtypeStruct((PEER_HEADS, t // V7X_LANES, PEER_NKEYS // 2, V7X_LANES), jnp.uint32)
    blocked_spec = pl.BlockSpec((PEER_HEADS, tm // V7X_LANES, PEER_NKEYS // 2, V7X_LANES), lambda i: (0, i, 0, 0))
    return pl.pallas_call(
        _peer_prep_kernel,
        grid=(t // tm,),
        in_specs=[
            pl.BlockSpec((tm, d), lambda i: (i, 0)),
            pl.BlockSpec((1, d), lambda i: (0, 0)),
            pl.BlockSpec((nq, d), lambda i: (0, 0)),
            pl.BlockSpec(sub_keys.shape, lambda i: (0, 0, 0)),
        ],
        out_specs=[pl.BlockSpec((d, tm), lambda i: (0, i)), blocked_spec, blocked_spec, fac_spec, fac_spec],
        out_shape=[jax.ShapeDtypeStruct((d, t), BF16), blocked, blocked, fac, fac],
        scratch_shapes=[pltpu.VMEM((nq, tm), F32)],
        compiler_params=_params("parallel"),
        name="peer_prep",
    )(x, gain.reshape(1, d), wq_t, sub_keys)


PEER_KEYS_PER_STEP = 8
PEER_ETILE = PEER_KEYS_PER_STEP * PEER_NKEYS
PEER_SUBROWS = 16


def _peer_main_kernel(hnt_first_ref, u_first_ref, hnt_ref, u_ref, vt_ref, b2_ref, e2_ref, n1_ref, c_ref, x_ref,
                      o_ref, acc_ref, at_ref, p_ref, *, tiles_per_block):
    g = pl.program_id(0)
    j = lax.rem(g, tiles_per_block)
    tm = hnt_ref.shape[1]
    lane_groups = tm // V7X_LANES

    halves = 2
    keys_per_half = PEER_KEYS_PER_STEP // halves
    rows_per_half = PEER_ETILE // halves
    words = PEER_SUBROWS // 2
    word_tile = (words, V7X_LANES)

    def packed(x):
        return pltpu.bitcast(x, BF16)

    def pre_activation(u_tile_ref, hn_ref, slot):
        at_ref[slot] = jnp.dot(u_tile_ref[...], hn_ref[...], preferred_element_type=F32)

    @pl.when(g == 0)
    def _():
        pre_activation(u_first_ref, hnt_first_ref, 0)

    @pl.when(j == 0)
    def _():
        acc_ref[...] = jnp.zeros_like(acc_ref)

    def weighted_gelu(e, slot):
        for ka in range(e * keys_per_half, (e + 1) * keys_per_half):
            for lg in range(lane_groups):
                lanes = slice(lg * V7X_LANES, (lg + 1) * V7X_LANES)
                n1 = [packed(jnp.broadcast_to(n1_ref[h, ka:ka + 1, lanes], word_tile)) for h in range(PEER_HEADS)]
                cw = [packed(jnp.broadcast_to(c_ref[h, ka:ka + 1, lanes], word_tile)) for h in range(PEER_HEADS)]
                for bg in range(PEER_NKEYS // PEER_SUBROWS):
                    wrows = slice(bg * words, (bg + 1) * words)
                    w = None
                    for h in range(PEER_HEADS):
                        term = jnp.where(packed(b2_ref[h, lg, wrows, :]) < n1[h],
                                         packed(e2_ref[h, lg, wrows, :]) * cw[h], 0.0)
                        w = term if w is None else w + term
                    first = ka * PEER_NKEYS + bg * PEER_SUBROWS
                    pre = at_ref[slot, first:first + PEER_SUBROWS, lanes]
                    act = pre * (1.0 + lax.erf(pre * (2.0 ** -0.5)))
                    p_ref[lg, first // 2:first // 2 + words, :] = pltpu.bitcast(act.astype(BF16) * w, jnp.uint32)

    def value_matmul(e):
        rows = slice(e * rows_per_half, (e + 1) * rows_per_half)
        wrows = slice(e * rows_per_half // 2, (e + 1) * rows_per_half // 2)
        p_half = jnp.concatenate([packed(p_ref[lg, wrows, :]) for lg in range(lane_groups)], axis=-1)
        acc_ref[...] += jnp.dot(vt_ref[:, rows], p_half, preferred_element_type=F32)

    def step(slot):
        pre_activation(u_ref, hnt_ref, 1 - slot)
        weighted_gelu(0, slot)
        value_matmul(0)
        weighted_gelu(1, slot)
        value_matmul(1)

    for parity in range(2):
        pl.when(lax.rem(g, 2) == parity)(functools.partial(step, parity))

    @pl.when(j == tiles_per_block - 1)
    def _():
        o_ref[...] = x_ref[...] + acc_ref[...].T


def peer_main(hn_t, u, v_t, b2, e2, n1, c, x, *, tm=512):
    t, d = x.shape
    n_exp = u.shape[0]
    assert v_t.shape == (n_exp // PEER_ETILE, d, PEER_ETILE)
    tm = min(tm, t)
    assert t % tm == 0 and tm % V7X_LANES == 0
    assert n_exp % PEER_ETILE == 0 and n_exp == PEER_NKEYS * PEER_NKEYS
    lane_groups = tm // V7X_LANES
    n_i = t // tm
    n_j = n_exp // PEER_ETILE
    assert n_j % 2 == 0

    def block(g):
        return g // n_j

    def tile(g):
        return lax.rem(g, n_j)

    def nxt(g):
        return jnp.minimum(g + 1, n_i * n_j - 1)

    all_keys = pl.BlockSpec((PEER_HEADS, lane_groups, PEER_NKEYS // 2, V7X_LANES), lambda g: (0, block(g), 0, 0))
    step_keys = pl.BlockSpec((PEER_HEADS, PEER_KEYS_PER_STEP, tm), lambda g: (0, tile(g), block(g)))
    return pl.pallas_call(
        functools.partial(_peer_main_kernel, tiles_per_block=n_j),
        grid=(n_i * n_j,),
        in_specs=[
            pl.BlockSpec((d, tm), lambda g: (0, 0), pipeline_mode=pl.Buffered(1)),
            pl.BlockSpec((PEER_ETILE, d), lambda g: (0, 0), pipeline_mode=pl.Buffered(1)),
            pl.BlockSpec((d, tm), lambda g: (0, block(nxt(g)))),
            pl.BlockSpec((PEER_ETILE, d), lambda g: (tile(nxt(g)), 0)),
            pl.BlockSpec((None, d, PEER_ETILE), lambda g: (tile(g), 0, 0)),
            all_keys, all_keys, step_keys, step_keys,
            pl.BlockSpec((tm, d), lambda g: (block(g), 0)),
        ],
        out_specs=pl.BlockSpec((tm, d), lambda g: (block(g), 0)),
        out_shape=jax.ShapeDtypeStruct((t, d), F32),
        scratch_shapes=[
            pltpu.VMEM((d, tm), F32),
            pltpu.VMEM((2, PEER_ETILE, tm), F32),
            pltpu.VMEM((lane_groups, PEER_ETILE // 2, V7X_LANES), jnp.uint32),
        ],
        compiler_params=_params("arbitrary"),
        name="peer_main",
    )(hn_t, u, hn_t, u, v_t, b2, e2, n1, c, x)


def peer_layer(x, gain, wq_t, sub_keys, u, v_t):
    hn_t, b2, e2, n1, c = peer_prep(x, gain, wq_t, sub_keys)
    return peer_main(hn_t, u, v_t, b2, e2, n1, c, x)


def _final_norm_kernel(x_ref, g_ref, o_ref):
    o_ref[...] = _rmsnorm_rows(x_ref[...], g_ref[...])


def final_norm(x, gain, *, tm=1024):
    t, d = x.shape
    tm = min(tm, t)
    return pl.pallas_call(
        _final_norm_kernel,
        grid=(t // tm,),
        in_specs=[pl.BlockSpec((tm, d), lambda i: (i, 0)), pl.BlockSpec((1, d), lambda i: (0, 0))],
        out_specs=pl.BlockSpec((tm, d), lambda i: (i, 0)),
        out_shape=jax.ShapeDtypeStruct((t, d), F32),
        compiler_params=_params("parallel"),
        name="final_norm",
    )(x, gain.reshape(1, d))


def kernel(x, ln_mix, ln_ffn, ret_w_in, ret_w_out, ret_gn, kv_norm, w_kv, att_w_q, att_w_o, att_rel_bias,
           peer_w_q, peer_sub_keys, peer_u, peer_v, ln_final):
    batch, seq, d = x.shape
    depth = ln_mix.shape[0]
    n_a = ret_w_in.shape[0]
    xt = x.reshape(batch * seq, d)
    qk_width = 2 * RET_HEADS * RET_QK_DIM
    v_width = RET_HEADS * RET_V_DIM
    kv = None
    w_in_all = ret_w_in.astype(BF16)
    for l in range(depth):
        if l < n_a:
            qk, v, g = norm_matmul_split(xt, ln_mix[l], w_in_all, l, (qk_width, v_width, v_width), (F32, BF16, F32))
            y = retention_core(qk, v, g, ret_gn[l], batch, seq)
            xt = matmul_residual(y, ret_w_out[l].astype(BF16), xt)
        else:
            j = l - n_a
            q = norm_matmul(xt, ln_mix[l], att_w_q[j].astype(BF16), col_start=0, n_cols=d, out_dtype=BF16,
                            scale=ATT_HEAD_DIM ** -0.5)
            a = attention_core(q, kv, _attention_bias(att_rel_bias[j]), batch, seq)
            xt = matmul_residual(a, att_w_o[j].astype(BF16), xt)
        xt = peer_layer(xt, ln_ffn[l], peer_w_q[l].T.astype(BF16), peer_sub_keys[l].astype(BF16),
                        peer_u[l].astype(BF16),
                        peer_v[l].reshape(-1, PEER_ETILE, d).transpose(0, 2, 1).astype(BF16))
        if l == n_a - 1:
            kv = norm_matmul(xt, kv_norm, w_kv.astype(BF16), col_start=0, n_cols=2 * d, out_dtype=BF16)
    return final_norm(xt, ln_final).reshape(batch, seq, d)
```

```python
import functools

import jax
import jax.numpy as jnp
from jax import lax
from jax.experimental import pallas as pl
from jax.experimental.pallas import tpu as pltpu

F32 = jnp.float32
BF16 = jnp.bfloat16

EPS = 1e-6
ROPE_BASE = 10000.0
REF_CHUNK = 64
RET_HEADS = 8
RET_QK_DIM = 256
RET_V_DIM = 512
RET_BLOCK = 512
ATT_HEADS = 16
ATT_HEAD_DIM = 128
LEFT_CHUNKS = 8
REL_CLIP = 128
ATT_QBLOCK = 256
ATT_WINDOW = ATT_QBLOCK + LEFT_CHUNKS * REF_CHUNK
ATT_HEAD_UNROLL = 4
PEER_HEADS = 8
PEER_NKEYS = 128
PEER_HALF = 128
PEER_TOPK = 16
MASK_VALUE = -1e30
NEG_INF = float("-inf")

V7X_LANES = 128
V7X_VMEM_LIMIT_BYTES = 60 * 1024 * 1024

NT_DIMS = (((1,), (1,)), ((), ()))
TN_DIMS = (((0,), (0,)), ((), ()))


def _params(*sem, flags=None):
    return pltpu.CompilerParams(dimension_semantics=sem, vmem_limit_bytes=V7X_VMEM_LIMIT_BYTES, flags=flags)


def _rmsnorm_rows(x, g):
    ms = jnp.mean(x * x, axis=-1, keepdims=True)
    return x * lax.rsqrt(ms + EPS) * g


def _norm_matmul_kernel(x_ref, g_ref, w_ref, o_ref, hn_ref, *, scale):
    @pl.when(pl.program_id(1) == 0)
    def _():
        hn_ref[...] = _rmsnorm_rows(x_ref[...], g_ref[...]).astype(BF16)

    acc = jnp.dot(hn_ref[...], w_ref[...], preferred_element_type=F32)
    if scale != 1.0:
        acc = acc * scale
    o_ref[...] = acc.astype(o_ref.dtype)


def norm_matmul(x, gain, w, *, col_start, n_cols, out_dtype, scale=1.0, tm=1024, tn=1024):
    t, d = x.shape
    tm = min(tm, t)
    tn = min(tn, n_cols)
    assert t % tm == 0 and n_cols % tn == 0 and col_start % tn == 0
    off = col_start // tn
    return pl.pallas_call(
        functools.partial(_norm_matmul_kernel, scale=scale),
        grid=(t // tm, n_cols // tn),
        in_specs=[
            pl.BlockSpec((tm, d), lambda i, j: (i, 0)),
            pl.BlockSpec((1, d), lambda i, j: (0, 0)),
            pl.BlockSpec((d, tn), lambda i, j: (0, j + off)),
        ],
        out_specs=pl.BlockSpec((tm, tn), lambda i, j: (i, j)),
        out_shape=jax.ShapeDtypeStruct((t, n_cols), out_dtype),
        scratch_shapes=[pltpu.VMEM((tm, d), BF16)],
        compiler_params=_params("parallel", "arbitrary"),
        name="norm_matmul",
    )(x, gain.reshape(1, d), w)


def _norm_matmul_split_kernel(x_ref, g_ref, w_ref, *refs, tile_bounds):
    out_refs, hn_ref = refs[:-1], refs[-1]
    j = pl.program_id(1)

    @pl.when(j == 0)
    def _():
        hn_ref[...] = _rmsnorm_rows(x_ref[...], g_ref[...]).astype(BF16)

    for o_ref, (lo, hi) in zip(out_refs, tile_bounds):
        @pl.when((j >= lo) & (j < hi))
        def _(o_ref=o_ref):
            o_ref[...] = jnp.dot(hn_ref[...], w_ref[...], preferred_element_type=F32).astype(o_ref.dtype)


def norm_matmul_split(x, gain, w, layer, widths, dtypes, *, tm=1024, tn=1024):
    t, d = x.shape
    tm = min(tm, t)
    assert t % tm == 0 and all(width % tn == 0 for width in widths) and sum(widths) == w.shape[-1]
    tile_bounds = []
    lo = 0
    for width in widths:
        tile_bounds.append((lo, lo + width // tn))
        lo += width // tn

    def out_spec(lo, hi):
        return pl.BlockSpec((tm, tn), lambda i, j: (i, jnp.clip(j - lo, 0, hi - lo - 1)))

    return pl.pallas_call(
        functools.partial(_norm_matmul_split_kernel, tile_bounds=tuple(tile_bounds)),
        grid=(t // tm, lo),
        in_specs=[
            pl.BlockSpec((tm, d), lambda i, j: (i, 0)),
            pl.BlockSpec((1, d), lambda i, j: (0, 0)),
            pl.BlockSpec((None, d, tn), lambda i, j: (layer, 0, j)),
        ],
        out_specs=[out_spec(*b) for b in tile_bounds],
        out_shape=[jax.ShapeDtypeStruct((t, width), dt) for width, dt in zip(widths, dtypes)],
        scratch_shapes=[pltpu.VMEM((tm, d), BF16)],
        compiler_params=_params("parallel", "arbitrary"),
        name="norm_matmul_split",
    )(x, gain.reshape(1, d), w)


def _matmul_residual_kernel(y_ref, w_ref, x_ref, o_ref):
    o_ref[...] = x_ref[...] + jnp.dot(y_ref[...], w_ref[...], preferred_element_type=F32)


MATMUL_WEIGHT_TILE_BYTES = 4 * 1024 * 1024


def matmul_residual(y, w, x, *, tm=1024):
    t, k = y.shape
    n = w.shape[1]
    tm = min(tm, t)
    tn = min(MATMUL_WEIGHT_TILE_BYTES // (2 * k), n)
    assert t % tm == 0 and n % tn == 0
    return pl.pallas_call(
        _matmul_residual_kernel,
        grid=(t // tm, n // tn),
        in_specs=[
            pl.BlockSpec((tm, k), lambda i, j: (i, 0)),
            pl.BlockSpec((k, tn), lambda i, j: (0, j)),
            pl.BlockSpec((tm, tn), lambda i, j: (i, j)),
        ],
        out_specs=pl.BlockSpec((tm, tn), lambda i, j: (i, j)),
        out_shape=jax.ShapeDtypeStruct((t, n), F32),
        compiler_params=_params("parallel", "arbitrary"),
        name="matmul_residual",
    )(y, w, x)


def _retention_tables(seq):
    c = RET_BLOCK
    log_gamma = jnp.log1p(-jnp.exp2(-5.0 - jnp.arange(RET_HEADS, dtype=F32)))
    idx = jnp.arange(c, dtype=F32)
    diff = idx[:, None] - idx[None, :]
    dmask = jnp.where(diff >= 0, jnp.exp(log_gamma[:, None, None] * jnp.maximum(diff, 0.0)), 0.0)
    xi = jnp.exp(log_gamma[:, None] * (idx[None, :] + 1.0))
    zeta = jnp.exp(log_gamma[:, None] * (c - 1.0 - idx[None, :]))
    cdecay = jnp.exp(log_gamma * c)
    xi = jnp.broadcast_to(xi[:, :, None], (RET_HEADS, c, V7X_LANES))
    zeta = jnp.broadcast_to(zeta[:, :, None], (RET_HEADS, c, V7X_LANES))
    cdecay = jnp.broadcast_to(cdecay[:, None, None], (RET_HEADS, 1, RET_V_DIM))
    half = RET_QK_DIM // 2
    inv = 1.0 / (ROPE_BASE ** (jnp.arange(half, dtype=F32) / half))
    ang = jnp.arange(seq, dtype=F32)[:, None] * inv[None, :]
    return dmask, xi, zeta, cdecay, jnp.cos(ang), jnp.sin(ang)


def _retention_kernel(q_ref, k_ref, v_ref, g_ref, cos_ref, sin_ref, dmask_ref, xi_ref, zeta_ref, cd_ref,
                      gn_ref, y_ref, state_ref):
    @pl.when(pl.program_id(2) == 0)
    def _():
        state_ref[...] = jnp.zeros_like(state_ref)

    cos = cos_ref[...]
    sin = sin_ref[...]
    half = RET_QK_DIM // 2

    def rotate(t):
        t1 = t[:, :half]
        t2 = t[:, half:]
        return jnp.concatenate([t1 * cos - t2 * sin, t1 * sin + t2 * cos], axis=-1)

    q = rotate(q_ref[...])
    k = rotate(k_ref[...]) * (RET_QK_DIM ** -0.5)
    qb = q.astype(BF16)
    kb = k.astype(BF16)
    v = v_ref[...]

    scores = lax.dot_general(qb, kb, NT_DIMS, preferred_element_type=F32) * dmask_ref[...]
    intra = jnp.dot(scores.astype(BF16), v, preferred_element_type=F32)
    state = state_ref[...]
    xi = jnp.concatenate([xi_ref[...]] * (RET_V_DIM // V7X_LANES), axis=-1)
    cross = jnp.dot(qb, state.astype(BF16), preferred_element_type=F32) * xi
    o = intra + cross

    zeta = jnp.concatenate([zeta_ref[...]] * (RET_QK_DIM // V7X_LANES), axis=-1)
    kz = (k * zeta).astype(BF16)
    state_ref[...] = state * cd_ref[...] + lax.dot_general(kz, v, TN_DIMS, preferred_element_type=F32)

    mu = jnp.mean(o, axis=-1, keepdims=True)
    d = o - mu
    var = jnp.mean(d * d, axis=-1, keepdims=True)
    y = d * lax.rsqrt(var + EPS) * gn_ref[...]
    g = g_ref[...]
    y_ref[...] = (g * jax.nn.sigmoid(g) * y).astype(y_ref.dtype)


def retention_core(qk, v, g, gn, batch, seq):
    c = RET_BLOCK
    nc = seq // c
    dmask, xi, zeta, cdecay, cos, sin = _retention_tables(seq)
    row = lambda b, h, n: (b * nc + n, h)
    per_head = lambda b, h, n: (h, 0, 0)
    return pl.pallas_call(
        _retention_kernel,
        grid=(batch, RET_HEADS, nc),
        in_specs=[
            pl.BlockSpec((c, RET_QK_DIM), row),
            pl.BlockSpec((c, RET_QK_DIM), lambda b, h, n: (b * nc + n, RET_HEADS + h)),
            pl.BlockSpec((c, RET_V_DIM), row),
            pl.BlockSpec((c, RET_V_DIM), row),
            pl.BlockSpec((c, RET_QK_DIM // 2), lambda b, h, n: (n, 0)),
            pl.BlockSpec((c, RET_QK_DIM // 2), lambda b, h, n: (n, 0)),
            pl.BlockSpec((None, c, c), per_head),
            pl.BlockSpec((None, c, V7X_LANES), per_head),
            pl.BlockSpec((None, c, V7X_LANES), per_head),
            pl.BlockSpec((None, 1, RET_V_DIM), per_head),
            pl.BlockSpec((1, RET_V_DIM), lambda b, h, n: (0, h)),
        ],
        out_specs=pl.BlockSpec((c, RET_V_DIM), row),
        out_shape=jax.ShapeDtypeStruct(v.shape, BF16),
        scratch_shapes=[pltpu.VMEM((RET_QK_DIM, RET_V_DIM), F32)],
        compiler_params=_params("parallel", "parallel", "arbitrary"),
        name="retention_core",
    )(qk, qk, v, g, cos, sin, dmask, xi, zeta, cdecay, gn.reshape(1, -1))


def _attention_bias(rel_bias):
    i = jnp.arange(ATT_QBLOCK)[:, None]
    m = jnp.arange(ATT_WINDOW)[None, :]
    period = ATT_QBLOCK + ATT_WINDOW - 1
    lag = jnp.arange(period)
    lag = jnp.where(lag < ATT_WINDOW, lag, lag - period)
    rel = jnp.clip(LEFT_CHUNKS * REF_CHUNK - lag, -REL_CLIP, REL_CLIP) + REL_CLIP
    first = rel_bias.astype(F32)[:, rel]
    heads = rel_bias.shape[0]
    skew = jnp.tile(first, (1, ATT_QBLOCK + 1))[:, :ATT_QBLOCK * (period - 1)]
    toeplitz = skew.reshape(heads, ATT_QBLOCK, period - 1)[:, :, :ATT_WINDOW]
    ci = i // REF_CHUNK
    cm = m // REF_CHUNK
    band = (cm >= ci) & (cm <= ci + LEFT_CHUNKS)
    return jnp.where(band[None], toeplitz, MASK_VALUE)


def _attention_kernel(q_ref, k0_ref, k1_ref, k2_ref, v0_ref, v1_ref, v2_ref, bias_ref, o_ref):
    qblk = pl.program_id(1)
    n_prev = LEFT_CHUNKS * REF_CHUNK // ATT_QBLOCK
    col = lax.broadcasted_iota(jnp.int32, (1, ATT_WINDOW), 1)
    start_mask = jnp.where(col >= (n_prev - qblk) * ATT_QBLOCK, 0.0, MASK_VALUE)
    k_refs = (k0_ref, k1_ref, k2_ref)
    v_refs = (v0_ref, v1_ref, v2_ref)

    def one_head(h, carry):
        cols = pl.ds(pl.multiple_of(h * ATT_HEAD_DIM, ATT_HEAD_DIM), ATT_HEAD_DIM)
        q = q_ref[:, cols]
        s = jnp.concatenate(
            [lax.dot_general(q, kr[:, cols], NT_DIMS, preferred_element_type=F32) for kr in k_refs], axis=-1)
        s = s + bias_ref[h] + start_mask
        m = jnp.max(s, axis=-1, keepdims=True)
        p = jnp.exp(s - m)
        p = (p / jnp.sum(p, axis=-1, keepdims=True)).astype(BF16)
        o = jnp.dot(p[:, :ATT_QBLOCK], v_refs[0][:, cols], preferred_element_type=F32)
        for w in range(1, len(v_refs)):
            o = o + jnp.dot(p[:, w * ATT_QBLOCK:(w + 1) * ATT_QBLOCK], v_refs[w][:, cols],
                            preferred_element_type=F32)
        o_ref[:, cols] = o.astype(o_ref.dtype)
        return carry

    lax.fori_loop(0, ATT_HEADS, one_head, 0, unroll=ATT_HEAD_UNROLL)


def attention_core(q, kv, bias, batch, seq):
    t, d = q.shape
    nq = seq // ATT_QBLOCK
    n_prev = LEFT_CHUNKS * REF_CHUNK // ATT_QBLOCK
    assert ATT_WINDOW == (n_prev + 1) * ATT_QBLOCK

    def window(w, part):
        return pl.BlockSpec((ATT_QBLOCK, d), lambda b, n: (b * nq + jnp.maximum(n - n_prev + w, 0), part))

    return pl.pallas_call(
        _attention_kernel,
        grid=(batch, nq),
        in_specs=[pl.BlockSpec((ATT_QBLOCK, d), lambda b, n: (b * nq + n, 0))]
        + [window(w, 0) for w in range(n_prev + 1)]
        + [window(w, 1) for w in range(n_prev + 1)]
        + [pl.BlockSpec(bias.shape, lambda b, n: (0, 0, 0))],
        out_specs=pl.BlockSpec((ATT_QBLOCK, d), lambda b, n: (b * nq + n, 0)),
        out_shape=jax.ShapeDtypeStruct((t, d), BF16),
        compiler_params=_params("parallel", "arbitrary"),
        name="attention_core",
    )(q, kv, kv, kv, kv, kv, kv, bias)


PEER_RANKS = PEER_TOPK + 1
V7X_SUBLANES = 8


def _sorting_network(n):
    pairs = []
    p = 1
    while p < n:
        k = p
        while k >= 1:
            for j in range(k % p, n - k, 2 * k):
                for i in range(min(k, n - j - k)):
                    if (i + j) // (2 * p) == (i + j + k) // (2 * p):
                        pairs.append((i + j, i + j + k))
            k //= 2
        p *= 2
    return pairs


def _sort_descending(vals):
    vals = list(vals)
    for i, j in _sorting_network(len(vals)):
        vals[i], vals[j] = jnp.maximum(vals[i], vals[j]), jnp.minimum(vals[i], vals[j])
    return vals


def _top_ranked(s):
    n = PEER_TOPK
    tiles = [s[r * V7X_SUBLANES:(r + 1) * V7X_SUBLANES, :] for r in range(s.shape[0] // V7X_SUBLANES)]
    assert len(tiles) == n
    top = _sort_descending(tiles)
    shift = V7X_SUBLANES // 2
    while shift >= 1:
        other = [pltpu.roll(v, shift, 0) for v in top]
        top = [jnp.maximum(top[i], other[n - 1 - i]) for i in range(n)]
        k = n // 2
        while k >= 1:
            for i in range(n):
                if i & k == 0:
                    top[i], top[i + k] = jnp.maximum(top[i], top[i + k]), jnp.minimum(top[i], top[i + k])
            k //= 2
        shift //= 2
    below = [jnp.where(t < top[n - 1], t, NEG_INF) for t in tiles]
    while len(below) > 1:
        below = [jnp.maximum(below[2 * i], below[2 * i + 1]) for i in range(len(below) // 2)]
    nxt = below[0]
    shift = V7X_SUBLANES // 2
    while shift >= 1:
        nxt = jnp.maximum(nxt, pltpu.roll(nxt, shift, 0))
        shift //= 2
    return top + [nxt]


def _bf16_twice(x):
    hi = pltpu.bitcast(x.astype(BF16).astype(F32), jnp.uint32)
    return hi | (hi >> 16)


def _peer_prep_kernel(x_ref, g_ref, wqt_ref, sk_ref, hnt_ref, b2_ref, e2_ref, n1_ref, c_ref, qt_ref):
    tm = x_ref.shape[0]
    lane_groups = tm // V7X_LANES
    assert lane_groups <= V7X_SUBLANES
    hnt = _rmsnorm_rows(x_ref[...], g_ref[...]).T.astype(BF16)
    hnt_ref[...] = hnt
    qt_ref[...] = jnp.dot(wqt_ref[...], hnt, preferred_element_type=F32)
    sublane = lax.broadcasted_iota(jnp.int32, (V7X_SUBLANES, V7X_LANES), 0)

    def one_head(h, carry):
        scores = []
        ranked = []
        for part in range(2):
            rows = pl.ds(pl.multiple_of(h * (2 * PEER_HALF) + part * PEER_HALF, PEER_HALF), PEER_HALF)
            s = jnp.dot(sk_ref[part], qt_ref[rows, :].astype(BF16), preferred_element_type=F32)
            scores.append(s)
            per_group = [_top_ranked(s[:, g * V7X_LANES:(g + 1) * V7X_LANES]) for g in range(lane_groups)]
            dense = []
            for r in range(PEER_RANKS):
                tile = per_group[lane_groups - 1][r]
                for g in range(lane_groups - 2, -1, -1):
                    tile = jnp.where(sublane == g, per_group[g][r], tile)
                dense.append(tile)
            ranked.append(dense)

        xs, ys = ranked
        sums = [xs[i] + ys[j] for i in range(PEER_RANKS) for j in range(PEER_RANKS) if (i + 1) * (j + 1) <= PEER_RANKS]
        size = 1
        while size < len(sums):
            size *= 2
        sums = _sort_descending(sums + [jnp.full_like(sums[0], NEG_INF)] * (size - len(sums)))
        z = jnp.ones_like(sums[0])
        for v in sums[1:PEER_TOPK]:
            z = z + jnp.exp(v - sums[0])
        tau = 0.5 * (sums[PEER_TOPK - 1] + sums[PEER_TOPK])
        half_over_z = 0.5 / z

        s1, s2 = scores
        for g in range(lane_groups):
            lanes = slice(g * V7X_LANES, (g + 1) * V7X_LANES)
            row = lambda t, g=g: t[g:g + 1, :]
            s1_g = s1[:, lanes]
            s2_g = s2[:, lanes]
            thr = row(tau) - s1_g
            b2 = jnp.zeros_like(s2_g)
            n1 = jnp.zeros_like(s1_g)
            for r in range(PEER_TOPK):
                y = row(ys[r])
                b2 = b2 + jnp.where(y > s2_g, 1.0, 0.0)
                n1 = n1 + jnp.where(y >= thr, 1.0, 0.0)
            c = jnp.exp(s1_g - row(xs[0])) * row(half_over_z)
            n1_ref[h, :, lanes] = _bf16_twice(n1)
            c_ref[h, :, lanes] = _bf16_twice(c)
            b2_ref[h, g] = pltpu.bitcast(b2.astype(BF16), jnp.uint32)
            e2_ref[h, g] = pltpu.bitcast(jnp.exp(s2_g - row(ys[0])).astype(BF16), jnp.uint32)
        return carry

    lax.fori_loop(0, PEER_HEADS, one_head, 0)


def peer_prep(x, gain, wq_t, sub_keys, *, tm=512):
    t, d = x.shape
    tm = min(tm, t)
    assert t % tm == 0 and tm % V7X_LANES == 0
    nq = wq_t.shape[0]
    fac = jax.ShapeDtypeStruct((PEER_HEADS, PEER_NKEYS, t), jnp.uint32)
    fac_spec = pl.BlockSpec((PEER_HEADS, PEER_NKEYS, tm), lambda i: (0, 0, i))
    blocked = jax.ShapeDtypeStruct((PEER_HEADS, t // V7X_LANES, PEER_NKEYS // 2, V7X_LANES), jnp.uint32)
    blocked_spec = pl.BlockSpec((PEER_HEADS, tm // V7X_LANES, PEER_NKEYS // 2, V7X_LANES), lambda i: (0, i, 0, 0))
    return pl.pallas_call(
        _peer_prep_kernel,
        grid=(t // tm,),
        in_specs=[
            pl.BlockSpec((tm, d), lambda i: (i, 0)),
            pl.BlockSpec((1, d), lambda i: (0, 0)),
            pl.BlockSpec((nq, d), lambda i: (0, 0)),
            pl.BlockSpec(sub_keys.shape, lambda i: (0, 0, 0)),
        ],
        out_specs=[pl.BlockSpec((d, tm), lambda i: (0, i)), blocked_spec, blocked_spec, fac_spec, fac_spec],
        out_shape=[jax.ShapeDtypeStruct((d, t), BF16), blocked, blocked, fac, fac],
        scratch_shapes=[pltpu.VMEM((nq, tm), F32)],
        compiler_params=_params("parallel"),
        name="peer_prep",
    )(x, gain.reshape(1, d), wq_t, sub_keys)


PEER_KEYS_PER_STEP = 8
PEER_ETILE = PEER_KEYS_PER_STEP * PEER_NKEYS
PEER_SUBROWS = 16


def _peer_main_kernel(hnt_ref, u_ref, vt_ref, b2_ref, e2_ref, n1_ref, c_ref, x_ref, o_ref, acc_ref, at_ref, p_ref):
    j = pl.program_id(1)
    tm = hnt_ref.shape[1]
    lane_groups = tm // V7X_LANES

    halves = 2
    keys_per_half = PEER_KEYS_PER_STEP // halves
    rows_per_half = PEER_ETILE // halves
    words = PEER_SUBROWS // 2
    word_tile = (words, V7X_LANES)

    def packed(x):
        return pltpu.bitcast(x, BF16)

    @pl.when(j == 0)
    def _():
        acc_ref[...] = jnp.zeros_like(acc_ref)

    def pre_activation(e):
        rows = slice(e * rows_per_half, (e + 1) * rows_per_half)
        pre_all = jnp.dot(u_ref[rows, :], hnt_ref[...], preferred_element_type=F32)
        for lg in range(lane_groups):
            at_ref[lg, rows, :] = pre_all[:, lg * V7X_LANES:(lg + 1) * V7X_LANES]

    def weighted_gelu(e):
        for ka in range(e * keys_per_half, (e + 1) * keys_per_half):
            for lg in range(lane_groups):
                lanes = slice(lg * V7X_LANES, (lg + 1) * V7X_LANES)
                n1 = [packed(jnp.broadcast_to(n1_ref[h, ka:ka + 1, lanes], word_tile)) for h in range(PEER_HEADS)]
                cw = [packed(jnp.broadcast_to(c_ref[h, ka:ka + 1, lanes], word_tile)) for h in range(PEER_HEADS)]
                for bg in range(PEER_NKEYS // PEER_SUBROWS):
                    wrows = slice(bg * words, (bg + 1) * words)
                    w = None
                    for h in range(PEER_HEADS):
                        term = jnp.where(packed(b2_ref[h, lg, wrows, :]) < n1[h],
                                         packed(e2_ref[h, lg, wrows, :]) * cw[h], 0.0)
                        w = term if w is None else w + term
                    first = ka * PEER_NKEYS + bg * PEER_SUBROWS
                    pre = at_ref[lg, first:first + PEER_SUBROWS, :]
                    act = pre * (1.0 + lax.erf(pre * (2.0 ** -0.5)))
                    p_ref[lg, first // 2:first // 2 + words, :] = pltpu.bitcast(act.astype(BF16) * w, jnp.uint32)

    def value_matmul(e):
        rows = slice(e * rows_per_half, (e + 1) * rows_per_half)
        wrows = slice(e * rows_per_half // 2, (e + 1) * rows_per_half // 2)
        p_half = jnp.concatenate([packed(p_ref[lg, wrows, :]) for lg in range(lane_groups)], axis=-1)
        acc_ref[...] += jnp.dot(vt_ref[:, rows], p_half, preferred_element_type=F32)

    pre_activation(0)
    pre_activation(1)
    weighted_gelu(0)
    value_matmul(0)
    weighted_gelu(1)
    value_matmul(1)

    @pl.when(j == pl.num_programs(1) - 1)
    def _():
        o_ref[...] = x_ref[...] + acc_ref[...].T


def peer_main(hn_t, u, v_t, b2, e2, n1, c, x, *, tm=512):
    t, d = x.shape
    n_exp = u.shape[0]
    assert v_t.shape == (n_exp // PEER_ETILE, d, PEER_ETILE)
    tm = min(tm, t)
    assert t % tm == 0 and tm % V7X_LANES == 0
    assert n_exp % PEER_ETILE == 0 and n_exp == PEER_NKEYS * PEER_NKEYS
    lane_groups = tm // V7X_LANES
    all_keys = pl.BlockSpec((PEER_HEADS, lane_groups, PEER_NKEYS // 2, V7X_LANES), lambda i, j: (0, i, 0, 0))
    step_keys = pl.BlockSpec((PEER_HEADS, PEER_KEYS_PER_STEP, tm), lambda i, j: (0, j, i))
    return pl.pallas_call(
        _peer_main_kernel,
        grid=(t // tm, n_exp // PEER_ETILE),
        in_specs=[
            pl.BlockSpec((d, tm), lambda i, j: (0, i)),
            pl.BlockSpec((PEER_ETILE, d), lambda i, j: (j, 0)),
            pl.BlockSpec((None, d, PEER_ETILE), lambda i, j: (j, 0, 0)),
            all_keys, all_keys, step_keys, step_keys,
            pl.BlockSpec((tm, d), lambda i, j: (i, 0)),
        ],
        out_specs=pl.BlockSpec((tm, d), lambda i, j: (i, 0)),
        out_shape=jax.ShapeDtypeStruct((t, d), F32),
        scratch_shapes=[
            pltpu.VMEM((d, tm), F32),
            pltpu.VMEM((lane_groups, PEER_ETILE, V7X_LANES), F32),
            pltpu.VMEM((lane_groups, PEER_ETILE // 2, V7X_LANES), jnp.uint32),
        ],
        compiler_params=_params("parallel", "arbitrary"),
        name="peer_main",
    )(hn_t, u, v_t, b2, e2, n1, c, x)


def peer_layer(x, gain, wq_t, sub_keys, u, v_t):
    hn_t, b2, e2, n1, c = peer_prep(x, gain, wq_t, sub_keys)
    return peer_main(hn_t, u, v_t, b2, e2, n1, c, x)


def _final_norm_kernel(x_ref, g_ref, o_ref):
    o_ref[...] = _rmsnorm_rows(x_ref[...], g_ref[...])


def final_norm(x, gain, *, tm=1024):
    t, d = x.shape
    tm = min(tm, t)
    return pl.pallas_call(
        _final_norm_kernel,
        grid=(t // tm,),
        in_specs=[pl.BlockSpec((tm, d), lambda i: (i, 0)), pl.BlockSpec((1, d), lambda i: (0, 0))],
        out_specs=pl.BlockSpec((tm, d), lambda i: (i, 0)),
        out_shape=jax.ShapeDtypeStruct((t, d), F32),
        compiler_params=_params("parallel"),
        name="final_norm",
    )(x, gain.reshape(1, d))


def kernel(x, ln_mix, ln_ffn, ret_w_in, ret_w_out, ret_gn, kv_norm, w_kv, att_w_q, att_w_o, att_rel_bias,
           peer_w_q, peer_sub_keys, peer_u, peer_v, ln_final):
    batch, seq, d = x.shape
    depth = ln_mix.shape[0]
    n_a = ret_w_in.shape[0]
    xt = x.reshape(batch * seq, d)
    qk_width = 2 * RET_HEADS * RET_QK_DIM
    v_width = RET_HEADS * RET_V_DIM
    kv = None
    w_in_all = ret_w_in.astype(BF16)
    for l in range(depth):
        if l < n_a:
            qk, v, g = norm_matmul_split(xt, ln_mix[l], w_in_all, l, (qk_width, v_width, v_width), (F32, BF16, F32))
            y = retention_core(qk, v, g, ret_gn[l], batch, seq)
            xt = matmul_residual(y, ret_w_out[l].astype(BF16), xt)
        else:
            j = l - n_a
            q = norm_matmul(xt, ln_mix[l], att_w_q[j].astype(BF16), col_start=0, n_cols=d, out_dtype=BF16,
                            scale=ATT_HEAD_DIM ** -0.5)
            a = attention_core(q, kv, _attention_bias(att_rel_bias[j]), batch, seq)
            xt = matmul_residual(a, att_w_o[j].astype(BF16), xt)
        xt = peer_layer(xt, ln_ffn[l], peer_w_q[l].T.astype(BF16), peer_sub_keys[l].astype(BF16),
                        peer_u[l].astype(BF16),
                        peer_v[l].reshape(-1, PEER_ETILE, d).transpose(0, 2, 1).astype(BF16))
        if l == n_a - 1:
            kv = norm_matmul(xt, kv_norm, w_kv.astype(BF16), col_start=0, n_cols=2 * d, out_dtype=BF16)
    return final_norm(xt, ln_final).reshape(batch, seq, d)
```

```python
import functools

import jax
import jax.numpy as jnp
from jax import lax
from jax.experimental import pallas as pl
from jax.experimental.pallas import tpu as pltpu

F32 = jnp.float32
BF16 = jnp.bfloat16

EPS = 1e-6
ROPE_BASE = 10000.0
REF_CHUNK = 64
RET_HEADS = 8
RET_QK_DIM = 256
RET_V_DIM = 512
RET_BLOCK = 512
ATT_HEADS = 16
ATT_HEAD_DIM = 128
LEFT_CHUNKS = 8
REL_CLIP = 128
ATT_QBLOCK = 256
ATT_WINDOW = ATT_QBLOCK + LEFT_CHUNKS * REF_CHUNK
ATT_HEAD_UNROLL = 8
PEER_HEADS = 8
PEER_NKEYS = 128
PEER_HALF = 128
PEER_TOPK = 16
MASK_VALUE = -1e30
NEG_INF = float("-inf")

V7X_LANES = 128
V7X_VMEM_LIMIT_BYTES = 60 * 1024 * 1024

NT_DIMS = (((1,), (1,)), ((), ()))
TN_DIMS = (((0,), (0,)), ((), ()))


def _params(*sem, flags=None):
    return pltpu.CompilerParams(dimension_semantics=sem, vmem_limit_bytes=V7X_VMEM_LIMIT_BYTES, flags=flags)


def _rmsnorm_rows(x, g):
    ms = jnp.mean(x * x, axis=-1, keepdims=True)
    return x * lax.rsqrt(ms + EPS) * g


def _norm_matmul_kernel(x_ref, g_ref, w_ref, o_ref, hn_ref, *, scale):
    @pl.when(pl.program_id(1) == 0)
    def _():
        hn_ref[...] = _rmsnorm_rows(x_ref[...], g_ref[...]).astype(BF16)

    acc = jnp.dot(hn_ref[...], w_ref[...], preferred_element_type=F32)
    if scale != 1.0:
        acc = acc * scale
    o_ref[...] = acc.astype(o_ref.dtype)


def norm_matmul(x, gain, w, *, col_start, n_cols, out_dtype, scale=1.0, tm=1024, tn=1024):
    t, d = x.shape
    tm = min(tm, t)
    tn = min(tn, n_cols)
    assert t % tm == 0 and n_cols % tn == 0 and col_start % tn == 0
    off = col_start // tn
    return pl.pallas_call(
        functools.partial(_norm_matmul_kernel, scale=scale),
        grid=(t // tm, n_cols // tn),
        in_specs=[
            pl.BlockSpec((tm, d), lambda i, j: (i, 0)),
            pl.BlockSpec((1, d), lambda i, j: (0, 0)),
            pl.BlockSpec((d, tn), lambda i, j: (0, j + off)),
        ],
        out_specs=pl.BlockSpec((tm, tn), lambda i, j: (i, j)),
        out_shape=jax.ShapeDtypeStruct((t, n_cols), out_dtype),
        scratch_shapes=[pltpu.VMEM((tm, d), BF16)],
        compiler_params=_params("parallel", "arbitrary"),
        name="norm_matmul",
    )(x, gain.reshape(1, d), w)


def _norm_matmul_split_kernel(x_ref, g_ref, w_ref, *refs, tile_bounds):
    out_refs, hn_ref = refs[:-1], refs[-1]
    j = pl.program_id(1)

    @pl.when(j == 0)
    def _():
        hn_ref[...] = _rmsnorm_rows(x_ref[...], g_ref[...]).astype(BF16)

    for o_ref, (lo, hi) in zip(out_refs, tile_bounds):
        @pl.when((j >= lo) & (j < hi))
        def _(o_ref=o_ref):
            o_ref[...] = jnp.dot(hn_ref[...], w_ref[...], preferred_element_type=F32).astype(o_ref.dtype)


def norm_matmul_split(x, gain, w, layer, widths, dtypes, *, tm=1024, tn=1024):
    t, d = x.shape
    tm = min(tm, t)
    assert t % tm == 0 and all(width % tn == 0 for width in widths) and sum(widths) == w.shape[-1]
    tile_bounds = []
    lo = 0
    for width in widths:
        tile_bounds.append((lo, lo + width // tn))
        lo += width // tn

    def out_spec(lo, hi):
        return pl.BlockSpec((tm, tn), lambda i, j: (i, jnp.clip(j - lo, 0, hi - lo - 1)))

    return pl.pallas_call(
        functools.partial(_norm_matmul_split_kernel, tile_bounds=tuple(tile_bounds)),
        grid=(t // tm, lo),
        in_specs=[
            pl.BlockSpec((tm, d), lambda i, j: (i, 0)),
            pl.BlockSpec((1, d), lambda i, j: (0, 0)),
            pl.BlockSpec((None, d, tn), lambda i, j: (layer, 0, j)),
        ],
        out_specs=[out_spec(*b) for b in tile_bounds],
        out_shape=[jax.ShapeDtypeStruct((t, width), dt) for width, dt in zip(widths, dtypes)],
        scratch_shapes=[pltpu.VMEM((tm, d), BF16)],
        compiler_params=_params("parallel", "arbitrary"),
        name="norm_matmul_split",
    )(x, gain.reshape(1, d), w)


def _matmul_residual_kernel(y_ref, w_ref, x_ref, o_ref):
    o_ref[...] = x_ref[...] + jnp.dot(y_ref[...], w_ref[...], preferred_element_type=F32)


MATMUL_WEIGHT_TILE_BYTES = 4 * 1024 * 1024


def matmul_residual(y, w, x, *, tm=1024):
    t, k = y.shape
    n = w.shape[1]
    tm = min(tm, t)
    tn = min(MATMUL_WEIGHT_TILE_BYTES // (2 * k), n)
    assert t % tm == 0 and n % tn == 0
    return pl.pallas_call(
        _matmul_residual_kernel,
        grid=(t // tm, n // tn),
        in_specs=[
            pl.BlockSpec((tm, k), lambda i, j: (i, 0)),
            pl.BlockSpec((k, tn), lambda i, j: (0, j)),
            pl.BlockSpec((tm, tn), lambda i, j: (i, j)),
        ],
        out_specs=pl.BlockSpec((tm, tn), lambda i, j: (i, j)),
        out_shape=jax.ShapeDtypeStruct((t, n), F32),
        compiler_params=_params("parallel", "arbitrary"),
        name="matmul_residual",
    )(y, w, x)


def _retention_tables(seq):
    c = RET_BLOCK
    log_gamma = jnp.log1p(-jnp.exp2(-5.0 - jnp.arange(RET_HEADS, dtype=F32)))
    idx = jnp.arange(c, dtype=F32)
    diff = idx[:, None] - idx[None, :]
    dmask = jnp.where(diff >= 0, jnp.exp(log_gamma[:, None, None] * jnp.maximum(diff, 0.0)), 0.0)
    xi = jnp.exp(log_gamma[:, None] * (idx[None, :] + 1.0))
    zeta = jnp.exp(log_gamma[:, None] * (c - 1.0 - idx[None, :]))
    cdecay = jnp.exp(log_gamma * c)
    xi = jnp.broadcast_to(xi[:, :, None], (RET_HEADS, c, V7X_LANES))
    zeta = jnp.broadcast_to(zeta[:, :, None], (RET_HEADS, c, V7X_LANES))
    cdecay = jnp.broadcast_to(cdecay[:, None, None], (RET_HEADS, 1, RET_V_DIM))
    half = RET_QK_DIM // 2
    inv = 1.0 / (ROPE_BASE ** (jnp.arange(half, dtype=F32) / half))
    ang = jnp.arange(seq, dtype=F32)[:, None] * inv[None, :]
    return dmask, xi, zeta, cdecay, jnp.cos(ang), jnp.sin(ang)


def _retention_kernel(q_ref, k_ref, v_ref, g_ref, cos_ref, sin_ref, dmask_ref, xi_ref, zeta_ref, cd_ref,
                      gn_ref, y_ref, state_ref):
    @pl.when(pl.program_id(2) == 0)
    def _():
        state_ref[...] = jnp.zeros_like(state_ref)

    cos = cos_ref[...]
    sin = sin_ref[...]
    half = RET_QK_DIM // 2

    def rotate(t):
        t1 = t[:, :half]
        t2 = t[:, half:]
        return jnp.concatenate([t1 * cos - t2 * sin, t1 * sin + t2 * cos], axis=-1)

    q = rotate(q_ref[...])
    k = rotate(k_ref[...]) * (RET_QK_DIM ** -0.5)
    qb = q.astype(BF16)
    kb = k.astype(BF16)
    v = v_ref[...]

    scores = lax.dot_general(qb, kb, NT_DIMS, preferred_element_type=F32) * dmask_ref[...]
    intra = jnp.dot(scores.astype(BF16), v, preferred_element_type=F32)
    state = state_ref[...]
    xi = jnp.concatenate([xi_ref[...]] * (RET_V_DIM // V7X_LANES), axis=-1)
    cross = jnp.dot(qb, state.astype(BF16), preferred_element_type=F32) * xi
    o = intra + cross

    zeta = jnp.concatenate([zeta_ref[...]] * (RET_QK_DIM // V7X_LANES), axis=-1)
    kz = (k * zeta).astype(BF16)
    state_ref[...] = state * cd_ref[...] + lax.dot_general(kz, v, TN_DIMS, preferred_element_type=F32)

    mu = jnp.mean(o, axis=-1, keepdims=True)
    d = o - mu
    var = jnp.mean(d * d, axis=-1, keepdims=True)
    y = d * lax.rsqrt(var + EPS) * gn_ref[...]
    g = g_ref[...]
    y_ref[...] = (g * jax.nn.sigmoid(g) * y).astype(y_ref.dtype)


def retention_core(qk, v, g, gn, batch, seq):
    c = RET_BLOCK
    nc = seq // c
    dmask, xi, zeta, cdecay, cos, sin = _retention_tables(seq)
    row = lambda b, h, n: (b * nc + n, h)
    per_head = lambda b, h, n: (h, 0, 0)
    return pl.pallas_call(
        _retention_kernel,
        grid=(batch, RET_HEADS, nc),
        in_specs=[
            pl.BlockSpec((c, RET_QK_DIM), row),
            pl.BlockSpec((c, RET_QK_DIM), lambda b, h, n: (b * nc + n, RET_HEADS + h)),
            pl.BlockSpec((c, RET_V_DIM), row),
            pl.BlockSpec((c, RET_V_DIM), row),
            pl.BlockSpec((c, RET_QK_DIM // 2), lambda b, h, n: (n, 0)),
            pl.BlockSpec((c, RET_QK_DIM // 2), lambda b, h, n: (n, 0)),
            pl.BlockSpec((None, c, c), per_head),
            pl.BlockSpec((None, c, V7X_LANES), per_head),
            pl.BlockSpec((None, c, V7X_LANES), per_head),
            pl.BlockSpec((None, 1, RET_V_DIM), per_head),
            pl.BlockSpec((1, RET_V_DIM), lambda b, h, n: (0, h)),
        ],
        out_specs=pl.BlockSpec((c, RET_V_DIM), row),
        out_shape=jax.ShapeDtypeStruct(v.shape, BF16),
        scratch_shapes=[pltpu.VMEM((RET_QK_DIM, RET_V_DIM), F32)],
        compiler_params=_params("parallel", "parallel", "arbitrary"),
        name="retention_core",
    )(qk, qk, v, g, cos, sin, dmask, xi, zeta, cdecay, gn.reshape(1, -1))


def _attention_bias(rel_bias):
    i = jnp.arange(ATT_QBLOCK)[:, None]
    m = jnp.arange(ATT_WINDOW)[None, :]
    period = ATT_QBLOCK + ATT_WINDOW - 1
    lag = jnp.arange(period)
    lag = jnp.where(lag < ATT_WINDOW, lag, lag - period)
    rel = jnp.clip(LEFT_CHUNKS * REF_CHUNK - lag, -REL_CLIP, REL_CLIP) + REL_CLIP
    first = rel_bias.astype(F32)[:, rel]
    heads = rel_bias.shape[0]
    skew = jnp.tile(first, (1, ATT_QBLOCK + 1))[:, :ATT_QBLOCK * (period - 1)]
    toeplitz = skew.reshape(heads, ATT_QBLOCK, period - 1)[:, :, :ATT_WINDOW]
    ci = i // REF_CHUNK
    cm = m // REF_CHUNK
    band = (cm >= ci) & (cm <= ci + LEFT_CHUNKS)
    return jnp.where(band[None], toeplitz, MASK_VALUE)


def _attention_kernel(q_ref, k0_ref, k1_ref, k2_ref, v0_ref, v1_ref, v2_ref, bias_ref, o_ref):
    qblk = pl.program_id(1)
    n_prev = LEFT_CHUNKS * REF_CHUNK // ATT_QBLOCK
    col = lax.broadcasted_iota(jnp.int32, (1, ATT_WINDOW), 1)
    start_mask = jnp.where(col >= (n_prev - qblk) * ATT_QBLOCK, 0.0, MASK_VALUE)
    k_refs = (k0_ref, k1_ref, k2_ref)
    v_refs = (v0_ref, v1_ref, v2_ref)

    def one_head(h, carry):
        cols = pl.ds(pl.multiple_of(h * ATT_HEAD_DIM, ATT_HEAD_DIM), ATT_HEAD_DIM)
        q = q_ref[:, cols]
        s = jnp.concatenate(
            [lax.dot_general(q, kr[:, cols], NT_DIMS, preferred_element_type=F32) for kr in k_refs], axis=-1)
        s = s + bias_ref[h] + start_mask
        m = jnp.max(s, axis=-1, keepdims=True)
        p = jnp.exp(s - m)
        p = (p / jnp.sum(p, axis=-1, keepdims=True)).astype(BF16)
        o = jnp.dot(p[:, :ATT_QBLOCK], v_refs[0][:, cols], preferred_element_type=F32)
        for w in range(1, len(v_refs)):
            o = o + jnp.dot(p[:, w * ATT_QBLOCK:(w + 1) * ATT_QBLOCK], v_refs[w][:, cols],
                            preferred_element_type=F32)
        o_ref[:, cols] = o.astype(o_ref.dtype)
        return carry

    lax.fori_loop(0, ATT_HEADS, one_head, 0, unroll=ATT_HEAD_UNROLL)


def attention_core(q, kv, bias, batch, seq):
    t, d = q.shape
    nq = seq // ATT_QBLOCK
    n_prev = LEFT_CHUNKS * REF_CHUNK // ATT_QBLOCK
    assert ATT_WINDOW == (n_prev + 1) * ATT_QBLOCK

    def window(w, part):
        return pl.BlockSpec((ATT_QBLOCK, d), lambda b, n: (b * nq + jnp.maximum(n - n_prev + w, 0), part))

    return pl.pallas_call(
        _attention_kernel,
        grid=(batch, nq),
        in_specs=[pl.BlockSpec((ATT_QBLOCK, d), lambda b, n: (b * nq + n, 0))]
        + [window(w, 0) for w in range(n_prev + 1)]
        + [window(w, 1) for w in range(n_prev + 1)]
        + [pl.BlockSpec(bias.shape, lambda b, n: (0, 0, 0))],
        out_specs=pl.BlockSpec((ATT_QBLOCK, d), lambda b, n: (b * nq + n, 0)),
        out_shape=jax.ShapeDtypeStruct((t, d), BF16),
        compiler_params=_params("parallel", "arbitrary"),
        name="attention_core",
    )(q, kv, kv, kv, kv, kv, kv, bias)


PEER_RANKS = PEER_TOPK + 1
V7X_SUBLANES = 8


def _sorting_network(n):
    pairs = []
    p = 1
    while p < n:
        k = p
        while k >= 1:
            for j in range(k % p, n - k, 2 * k):
                for i in range(min(k, n - j - k)):
                    if (i + j) // (2 * p) == (i + j + k) // (2 * p):
                        pairs.append((i + j, i + j + k))
            k //= 2
        p *= 2
    return pairs


def _sort_descending(vals):
    vals = list(vals)
    for i, j in _sorting_network(len(vals)):
        vals[i], vals[j] = jnp.maximum(vals[i], vals[j]), jnp.minimum(vals[i], vals[j])
    return vals


def _top_ranked(s):
    n = PEER_TOPK
    tiles = [s[r * V7X_SUBLANES:(r + 1) * V7X_SUBLANES, :] for r in range(s.shape[0] // V7X_SUBLANES)]
    assert len(tiles) == n
    top = _sort_descending(tiles)
    shift = V7X_SUBLANES // 2
    while shift >= 1:
        other = [pltpu.roll(v, shift, 0) for v in top]
        top = [jnp.maximum(top[i], other[n - 1 - i]) for i in range(n)]
        k = n // 2
        while k >= 1:
            for i in range(n):
                if i & k == 0:
                    top[i], top[i + k] = jnp.maximum(top[i], top[i + k]), jnp.minimum(top[i], top[i + k])
            k //= 2
        shift //= 2
    below = [jnp.where(t < top[n - 1], t, NEG_INF) for t in tiles]
    while len(below) > 1:
        below = [jnp.maximum(below[2 * i], below[2 * i + 1]) for i in range(len(below) // 2)]
    nxt = below[0]
    shift = V7X_SUBLANES // 2
    while shift >= 1:
        nxt = jnp.maximum(nxt, pltpu.roll(nxt, shift, 0))
        shift //= 2
    return top + [nxt]


def _bf16_twice(x):
    hi = pltpu.bitcast(x.astype(BF16).astype(F32), jnp.uint32)
    return hi | (hi >> 16)


def _peer_prep_kernel(x_ref, g_ref, wqt_ref, sk_ref, hnt_ref, b2_ref, e2_ref, n1_ref, c_ref, qt_ref):
    tm = x_ref.shape[0]
    lane_groups = tm // V7X_LANES
    assert lane_groups <= V7X_SUBLANES
    hnt = _rmsnorm_rows(x_ref[...], g_ref[...]).T.astype(BF16)
    hnt_ref[...] = hnt
    qt_ref[...] = jnp.dot(wqt_ref[...], hnt, preferred_element_type=F32)
    sublane = lax.broadcasted_iota(jnp.int32, (V7X_SUBLANES, V7X_LANES), 0)

    def one_head(h, carry):
        scores = []
        ranked = []
        for part in range(2):
            rows = pl.ds(pl.multiple_of(h * (2 * PEER_HALF) + part * PEER_HALF, PEER_HALF), PEER_HALF)
            s = jnp.dot(sk_ref[part], qt_ref[rows, :].astype(BF16), preferred_element_type=F32)
            scores.append(s)
            per_group = [_top_ranked(s[:, g * V7X_LANES:(g + 1) * V7X_LANES]) for g in range(lane_groups)]
            dense = []
            for r in range(PEER_RANKS):
                tile = per_group[lane_groups - 1][r]
                for g in range(lane_groups - 2, -1, -1):
                    tile = jnp.where(sublane == g, per_group[g][r], tile)
                dense.append(tile)
            ranked.append(dense)

        xs, ys = ranked
        sums = [xs[i] + ys[j] for i in range(PEER_RANKS) for j in range(PEER_RANKS) if (i + 1) * (j + 1) <= PEER_RANKS]
        size = 1
        while size < len(sums):
            size *= 2
        sums = _sort_descending(sums + [jnp.full_like(sums[0], NEG_INF)] * (size - len(sums)))
        z = jnp.ones_like(sums[0])
        for v in sums[1:PEER_TOPK]:
            z = z + jnp.exp(v - sums[0])
        tau = 0.5 * (sums[PEER_TOPK - 1] + sums[PEER_TOPK])
        half_over_z = 0.5 / z

        passing = []
        for i in range(PEER_TOPK):
            thr_i = tau - xs[i]
            count = jnp.where(ys[0] >= thr_i, 1.0, 0.0)
            for r in range(1, PEER_TOPK):
                count = count + jnp.where(ys[r] >= thr_i, 1.0, 0.0)
            passing.append(count)
        s1, s2 = scores
        for g in range(lane_groups):
            lanes = slice(g * V7X_LANES, (g + 1) * V7X_LANES)
            row = lambda t, g=g: t[g:g + 1, :]
            s1_g = s1[:, lanes]
            s2_g = s2[:, lanes]
            b2 = jnp.full_like(s2_g, float(PEER_TOPK))
            n1 = jnp.zeros_like(s1_g)
            for r in reversed(range(PEER_TOPK)):
                b2 = jnp.where(s2_g == row(ys[r]), float(r), b2)
                n1 = jnp.where(s1_g == row(xs[r]), row(passing[r]), n1)
            c = jnp.exp(s1_g - row(xs[0])) * row(half_over_z)
            n1_ref[h, :, lanes] = _bf16_twice(n1)
            c_ref[h, :, lanes] = _bf16_twice(c)
            b2_ref[h, g] = pltpu.bitcast(b2.astype(BF16), jnp.uint32)
            e2_ref[h, g] = pltpu.bitcast(jnp.exp(s2_g - row(ys[0])).astype(BF16), jnp.uint32)
        return carry

    lax.fori_loop(0, PEER_HEADS, one_head, 0)


def peer_prep(x, gain, wq_t, sub_keys, *, tm=512):
    t, d = x.shape
    tm = min(tm, t)
    assert t % tm == 0 and tm % V7X_LANES == 0
    nq = wq_t.shape[0]
    fac = jax.ShapeDtypeStruct((PEER_HEADS, PEER_NKEYS, t), jnp.uint32)
    fac_spec = pl.BlockSpec((PEER_HEADS, PEER_NKEYS, tm), lambda i: (0, 0, i))
    blocked = jax.ShapeDtypeStruct((PEER_HEADS, t // V7X_LANES, PEER_NKEYS // 2, V7X_LANES), jnp.uint32)
    blocked_spec = pl.BlockSpec((PEER_HEADS, tm // V7X_LANES, PEER_NKEYS // 2, V7X_LANES), lambda i: (0, i, 0, 0))
    return pl.pallas_call(
        _peer_prep_kernel,
        grid=(t // tm,),
        in_specs=[
            pl.BlockSpec((tm, d), lambda i: (i, 0)),
            pl.BlockSpec((1, d), lambda i: (0, 0)),
            pl.BlockSpec((nq, d), lambda i: (0, 0)),
            pl.BlockSpec(sub_keys.shape, lambda i: (0, 0, 0)),
        ],
        out_specs=[pl.BlockSpec((d, tm), lambda i: (0, i)), blocked_spec, blocked_spec, fac_spec, fac_spec],
        out_shape=[jax.ShapeDtypeStruct((d, t), BF16), blocked, blocked, fac, fac],
        scratch_shapes=[pltpu.VMEM((nq, tm), F32)],
        compiler_params=_params("parallel"),
        name="peer_prep",
    )(x, gain.reshape(1, d), wq_t, sub_keys)


PEER_KEYS_PER_STEP = 8
PEER_ETILE = PEER_KEYS_PER_STEP * PEER_NKEYS
PEER_SUBROWS = 16


def _peer_main_kernel(hnt_first_ref, u_first_ref, hnt_ref, u_ref, vt_ref, b2_ref, e2_ref, n1_ref, c_ref, x_ref,
                      o_ref, acc_ref, at_ref, p_ref, *, tiles_per_block):
    g = pl.program_id(0)
    j = lax.rem(g, tiles_per_block)
    tm = hnt_ref.shape[1]
    lane_groups = tm // V7X_LANES

    halves = 2
    keys_per_half = PEER_KEYS_PER_STEP // halves
    rows_per_half = PEER_ETILE // halves
    words = PEER_SUBROWS // 2
    word_tile = (words, V7X_LANES)

    def packed(x):
        return pltpu.bitcast(x, BF16)

    def pre_activation(u_tile_ref, hn_ref, slot):
        at_ref[slot] = jnp.dot(u_tile_ref[...], hn_ref[...], preferred_element_type=F32)

    @pl.when(g == 0)
    def _():
        pre_activation(u_first_ref, hnt_first_ref, 0)

    @pl.when(j == 0)
    def _():
        acc_ref[...] = jnp.zeros_like(acc_ref)

    def weighted_gelu(e, slot):
        for ka in range(e * keys_per_half, (e + 1) * keys_per_half):
            for lg in range(lane_groups):
                lanes = slice(lg * V7X_LANES, (lg + 1) * V7X_LANES)
                n1 = [packed(jnp.broadcast_to(n1_ref[h, ka:ka + 1, lanes], word_tile)) for h in range(PEER_HEADS)]
                cw = [packed(jnp.broadcast_to(c_ref[h, ka:ka + 1, lanes], word_tile)) for h in range(PEER_HEADS)]
                for bg in range(PEER_NKEYS // PEER_SUBROWS):
                    wrows = slice(bg * words, (bg + 1) * words)
                    w = None
                    for h in range(PEER_HEADS):
                        term = jnp.where(packed(b2_ref[h, lg, wrows, :]) < n1[h],
                                         packed(e2_ref[h, lg, wrows, :]) * cw[h], 0.0)
                        w = term if w is None else w + term
                    first = ka * PEER_NKEYS + bg * PEER_SUBROWS
                    pre = at_ref[slot, first:first + PEER_SUBROWS, lanes]
                    act = pre * (1.0 + lax.erf(pre * (2.0 ** -0.5)))
                    p_ref[lg, first // 2:first // 2 + words, :] = pltpu.bitcast(act.astype(BF16) * w, jnp.uint32)

    def value_matmul(e):
        rows = slice(e * rows_per_half, (e + 1) * rows_per_half)
        wrows = slice(e * rows_per_half // 2, (e + 1) * rows_per_half // 2)
        p_half = jnp.concatenate([packed(p_ref[lg, wrows, :]) for lg in range(lane_groups)], axis=-1)
        acc_ref[...] += jnp.dot(vt_ref[:, rows], p_half, preferred_element_type=F32)

    def step(slot):
        pre_activation(u_ref, hnt_ref, 1 - slot)
        weighted_gelu(0, slot)
        value_matmul(0)
        weighted_gelu(1, slot)
        value_matmul(1)

    for parity in range(2):
        pl.when(lax.rem(g, 2) == parity)(functools.partial(step, parity))

    @pl.when(j == tiles_per_block - 1)
    def _():
        o_ref[...] = x_ref[...] + acc_ref[...].T


def peer_main(hn_t, u, v_t, b2, e2, n1, c, x, *, tm=512):
    t, d = x.shape
    n_exp = u.shape[0]
    assert v_t.shape == (n_exp // PEER_ETILE, d, PEER_ETILE)
    tm = min(tm, t)
    assert t % tm == 0 and tm % V7X_LANES == 0
    assert n_exp % PEER_ETILE == 0 and n_exp == PEER_NKEYS * PEER_NKEYS
    lane_groups = tm // V7X_LANES
    n_i = t // tm
    n_j = n_exp // PEER_ETILE
    assert n_j % 2 == 0

    def block(g):
        return g // n_j

    def tile(g):
        return lax.rem(g, n_j)

    def nxt(g):
        return jnp.minimum(g + 1, n_i * n_j - 1)

    all_keys = pl.BlockSpec((PEER_HEADS, lane_groups, PEER_NKEYS // 2, V7X_LANES), lambda g: (0, block(g), 0, 0))
    step_keys = pl.BlockSpec((PEER_HEADS, PEER_KEYS_PER_STEP, tm), lambda g: (0, tile(g), block(g)))
    return pl.pallas_call(
        functools.partial(_peer_main_kernel, tiles_per_block=n_j),
        grid=(n_i * n_j,),
        in_specs=[
            pl.BlockSpec((d, tm), lambda g: (0, 0), pipeline_mode=pl.Buffered(1)),
            pl.BlockSpec((PEER_ETILE, d), lambda g: (0, 0), pipeline_mode=pl.Buffered(1)),
            pl.BlockSpec((d, tm), lambda g: (0, block(nxt(g)))),
            pl.BlockSpec((PEER_ETILE, d), lambda g: (tile(nxt(g)), 0)),
            pl.BlockSpec((None, d, PEER_ETILE), lambda g: (tile(g), 0, 0)),
            all_keys, all_keys, step_keys, step_keys,
            pl.BlockSpec((tm, d), lambda g: (block(g), 0)),
        ],
        out_specs=pl.BlockSpec((tm, d), lambda g: (block(g), 0)),
        out_shape=jax.ShapeDtypeStruct((t, d), F32),
        scratch_shapes=[
            pltpu.VMEM((d, tm), F32),
            pltpu.VMEM((2, PEER_ETILE, tm), F32),
            pltpu.VMEM((lane_groups, PEER_ETILE // 2, V7X_LANES), jnp.uint32),
        ],
        compiler_params=_params("arbitrary"),
        name="peer_main",
    )(hn_t, u, hn_t, u, v_t, b2, e2, n1, c, x)


def peer_layer(x, gain, wq_t, sub_keys, u, v_t):
    hn_t, b2, e2, n1, c = peer_prep(x, gain, wq_t, sub_keys)
    return peer_main(hn_t, u, v_t, b2, e2, n1, c, x)


def _final_norm_kernel(x_ref, g_ref, o_ref):
    o_ref[...] = _rmsnorm_rows(x_ref[...], g_ref[...])


def final_norm(x, gain, *, tm=1024):
    t, d = x.shape
    tm = min(tm, t)
    return pl.pallas_call(
        _final_norm_kernel,
        grid=(t // tm,),
        in_specs=[pl.BlockSpec((tm, d), lambda i: (i, 0)), pl.BlockSpec((1, d), lambda i: (0, 0))],
        out_specs=pl.BlockSpec((tm, d), lambda i: (i, 0)),
        out_shape=jax.ShapeDtypeStruct((t, d), F32),
        compiler_params=_params("parallel"),
        name="final_norm",
    )(x, gain.reshape(1, d))


def kernel(x, ln_mix, ln_ffn, ret_w_in, ret_w_out, ret_gn, kv_norm, w_kv, att_w_q, att_w_o, att_rel_bias,
           peer_w_q, peer_sub_keys, peer_u, peer_v, ln_final):
    batch, seq, d = x.shape
    depth = ln_mix.shape[0]
    n_a = ret_w_in.shape[0]
    xt = x.reshape(batch * seq, d)
    qk_width = 2 * RET_HEADS * RET_QK_DIM
    v_width = RET_HEADS * RET_V_DIM
    kv = None
    w_in_all = ret_w_in.astype(BF16)
    for l in range(depth):
        if l < n_a:
            qk, v, g = norm_matmul_split(xt, ln_mix[l], w_in_all, l, (qk_width, v_width, v_width), (F32, BF16, F32))
            y = retention_core(qk, v, g, ret_gn[l], batch, seq)
            xt = matmul_residual(y, ret_w_out[l].astype(BF16), xt)
        else:
            j = l - n_a
            q = norm_matmul(xt, ln_mix[l], att_w_q[j].astype(BF16), col_start=0, n_cols=d, out_dtype=BF16,
                            scale=ATT_HEAD_DIM ** -0.5)
            a = attention_core(q, kv, _attention_bias(att_rel_bias[j]), batch, seq)
            xt = matmul_residual(a, att_w_o[j].astype(BF16), xt)
        xt = peer_layer(xt, ln_ffn[l], peer_w_q[l].T.astype(BF16), peer_sub_keys[l].astype(BF16),
                        peer_u[l].astype(BF16),
                        peer_v[l].reshape(-1, PEER_ETILE, d).transpose(0, 2, 1).astype(BF16))
        if l == n_a - 1:
            kv = norm_matmul(xt, kv_norm, w_kv.astype(BF16), col_start=0, n_cols=2 * d, out_dtype=BF16)
    return final_norm(xt, ln_final).reshape(batch, seq, d)
```

```python
import functools

import jax
import jax.numpy as jnp
from jax import lax
from jax.experimental import pallas as pl
from jax.experimental.pallas import tpu as pltpu

F32 = jnp.float32
BF16 = jnp.bfloat16

EPS = 1e-6
ROPE_BASE = 10000.0
REF_CHUNK = 64
RET_HEADS = 8
RET_QK_DIM = 256
RET_V_DIM = 512
RET_BLOCK = 512
ATT_HEADS = 16
ATT_HEAD_DIM = 128
LEFT_CHUNKS = 8
REL_CLIP = 128
ATT_QBLOCK = 256
ATT_WINDOW = ATT_QBLOCK + LEFT_CHUNKS * REF_CHUNK
ATT_HEAD_UNROLL = 8
PEER_HEADS = 8
PEER_NKEYS = 128
PEER_HALF = 128
PEER_TOPK = 16
MASK_VALUE = -1e30
NEG_INF = float("-inf")

V7X_LANES = 128
V7X_VMEM_LIMIT_BYTES = 60 * 1024 * 1024

NT_DIMS = (((1,), (1,)), ((), ()))
TN_DIMS = (((0,), (0,)), ((), ()))


def _params(*sem, flags=None):
    return pltpu.CompilerParams(dimension_semantics=sem, vmem_limit_bytes=V7X_VMEM_LIMIT_BYTES, flags=flags)


def _rmsnorm_rows(x, g):
    ms = jnp.mean(x * x, axis=-1, keepdims=True)
    return x * lax.rsqrt(ms + EPS) * g


def _norm_matmul_kernel(x_ref, g_ref, w_ref, o_ref, hn_ref, *, scale):
    @pl.when(pl.program_id(1) == 0)
    def _():
        hn_ref[...] = _rmsnorm_rows(x_ref[...], g_ref[...]).astype(BF16)

    acc = jnp.dot(hn_ref[...], w_ref[...], preferred_element_type=F32)
    if scale != 1.0:
        acc = acc * scale
    o_ref[...] = acc.astype(o_ref.dtype)


def norm_matmul(x, gain, w, *, col_start, n_cols, out_dtype, scale=1.0, tm=1024, tn=1024):
    t, d = x.shape
    tm = min(tm, t)
    tn = min(tn, n_cols)
    assert t % tm == 0 and n_cols % tn == 0 and col_start % tn == 0
    off = col_start // tn
    return pl.pallas_call(
        functools.partial(_norm_matmul_kernel, scale=scale),
        grid=(t // tm, n_cols // tn),
        in_specs=[
            pl.BlockSpec((tm, d), lambda i, j: (i, 0)),
            pl.BlockSpec((1, d), lambda i, j: (0, 0)),
            pl.BlockSpec((d, tn), lambda i, j: (0, j + off)),
        ],
        out_specs=pl.BlockSpec((tm, tn), lambda i, j: (i, j)),
        out_shape=jax.ShapeDtypeStruct((t, n_cols), out_dtype),
        scratch_shapes=[pltpu.VMEM((tm, d), BF16)],
        compiler_params=_params("parallel", "arbitrary"),
        name="norm_matmul",
    )(x, gain.reshape(1, d), w)


def _norm_matmul_split_kernel(x_ref, g_ref, w_ref, *refs, tile_bounds):
    out_refs, hn_ref = refs[:-1], refs[-1]
    j = pl.program_id(1)

    @pl.when(j == 0)
    def _():
        hn_ref[...] = _rmsnorm_rows(x_ref[...], g_ref[...]).astype(BF16)

    for o_ref, (lo, hi) in zip(out_refs, tile_bounds):
        @pl.when((j >= lo) & (j < hi))
        def _(o_ref=o_ref):
            o_ref[...] = jnp.dot(hn_ref[...], w_ref[...], preferred_element_type=F32).astype(o_ref.dtype)


def norm_matmul_split(x, gain, w, layer, widths, dtypes, *, tm=1024, tn=1024):
    t, d = x.shape
    tm = min(tm, t)
    assert t % tm == 0 and all(width % tn == 0 for width in widths) and sum(widths) == w.shape[-1]
    tile_bounds = []
    lo = 0
    for width in widths:
        tile_bounds.append((lo, lo + width // tn))
        lo += width // tn

    def out_spec(lo, hi):
        return pl.BlockSpec((tm, tn), lambda i, j: (i, jnp.clip(j - lo, 0, hi - lo - 1)))

    return pl.pallas_call(
        functools.partial(_norm_matmul_split_kernel, tile_bounds=tuple(tile_bounds)),
        grid=(t // tm, lo),
        in_specs=[
            pl.BlockSpec((tm, d), lambda i, j: (i, 0)),
            pl.BlockSpec((1, d), lambda i, j: (0, 0)),
            pl.BlockSpec((None, d, tn), lambda i, j: (layer, 0, j)),
        ],
        out_specs=[out_spec(*b) for b in tile_bounds],
        out_shape=[jax.ShapeDtypeStruct((t, width), dt) for width, dt in zip(widths, dtypes)],
        scratch_shapes=[pltpu.VMEM((tm, d), BF16)],
        compiler_params=_params("parallel", "arbitrary"),
        name="norm_matmul_split",
    )(x, gain.reshape(1, d), w)


def _matmul_residual_kernel(y_ref, w_ref, x_ref, o_ref):
    o_ref[...] = x_ref[...] + jnp.dot(y_ref[...], w_ref[...], preferred_element_type=F32)


MATMUL_WEIGHT_TILE_BYTES = 4 * 1024 * 1024


def matmul_residual(y, w, x, *, tm=1024):
    t, k = y.shape
    n = w.shape[1]
    tm = min(tm, t)
    tn = min(MATMUL_WEIGHT_TILE_BYTES // (2 * k), n)
    assert t % tm == 0 and n % tn == 0
    return pl.pallas_call(
        _matmul_residual_kernel,
        grid=(t // tm, n // tn),
        in_specs=[
            pl.BlockSpec((tm, k), lambda i, j: (i, 0)),
            pl.BlockSpec((k, tn), lambda i, j: (0, j)),
            pl.BlockSpec((tm, tn), lambda i, j: (i, j)),
        ],
        out_specs=pl.BlockSpec((tm, tn), lambda i, j: (i, j)),
        out_shape=jax.ShapeDtypeStruct((t, n), F32),
        compiler_params=_params("parallel", "arbitrary"),
        name="matmul_residual",
    )(y, w, x)


def _retention_tables(seq):
    c = RET_BLOCK
    log_gamma = jnp.log1p(-jnp.exp2(-5.0 - jnp.arange(RET_HEADS, dtype=F32)))
    idx = jnp.arange(c, dtype=F32)
    diff = idx[:, None] - idx[None, :]
    dmask = jnp.where(diff >= 0, jnp.exp(log_gamma[:, None, None] * jnp.maximum(diff, 0.0)), 0.0)
    xi = jnp.exp(log_gamma[:, None] * (idx[None, :] + 1.0))
    zeta = jnp.exp(log_gamma[:, None] * (c - 1.0 - idx[None, :]))
    cdecay = jnp.exp(log_gamma * c)
    xi = jnp.broadcast_to(xi[:, :, None], (RET_HEADS, c, V7X_LANES))
    zeta = jnp.broadcast_to(zeta[:, :, None], (RET_HEADS, c, V7X_LANES))
    cdecay = jnp.broadcast_to(cdecay[:, None, None], (RET_HEADS, 1, RET_V_DIM))
    half = RET_QK_DIM // 2
    inv = 1.0 / (ROPE_BASE ** (jnp.arange(half, dtype=F32) / half))
    ang = jnp.arange(seq, dtype=F32)[:, None] * inv[None, :]
    return dmask, xi, zeta, cdecay, jnp.cos(ang), jnp.sin(ang)


def _retention_kernel(q_ref, k_ref, v_ref, g_ref, cos_ref, sin_ref, dmask_ref, xi_ref, zeta_ref, cd_ref,
                      gn_ref, y_ref, state_ref):
    @pl.when(pl.program_id(2) == 0)
    def _():
        state_ref[...] = jnp.zeros_like(state_ref)

    cos = cos_ref[...]
    sin = sin_ref[...]
    half = RET_QK_DIM // 2

    def rotate(t):
        t1 = t[:, :half]
        t2 = t[:, half:]
        return jnp.concatenate([t1 * cos - t2 * sin, t1 * sin + t2 * cos], axis=-1)

    q = rotate(q_ref[...])
    k = rotate(k_ref[...]) * (RET_QK_DIM ** -0.5)
    qb = q.astype(BF16)
    kb = k.astype(BF16)
    v = v_ref[...]

    scores = lax.dot_general(qb, kb, NT_DIMS, preferred_element_type=F32) * dmask_ref[...]
    intra = jnp.dot(scores.astype(BF16), v, preferred_element_type=F32)
    state = state_ref[...]
    xi = jnp.concatenate([xi_ref[...]] * (RET_V_DIM // V7X_LANES), axis=-1)
    cross = jnp.dot(qb, state.astype(BF16), preferred_element_type=F32) * xi
    o = intra + cross

    zeta = jnp.concatenate([zeta_ref[...]] * (RET_QK_DIM // V7X_LANES), axis=-1)
    kz = (k * zeta).astype(BF16)
    state_ref[...] = state * cd_ref[...] + lax.dot_general(kz, v, TN_DIMS, preferred_element_type=F32)

    mu = jnp.mean(o, axis=-1, keepdims=True)
    d = o - mu
    var = jnp.mean(d * d, axis=-1, keepdims=True)
    y = d * lax.rsqrt(var + EPS) * gn_ref[...]
    g = g_ref[...]
    y_ref[...] = (g * jax.nn.sigmoid(g) * y).astype(y_ref.dtype)


def retention_core(qk, v, g, gn, batch, seq):
    c = RET_BLOCK
    nc = seq // c
    dmask, xi, zeta, cdecay, cos, sin = _retention_tables(seq)
    row = lambda b, h, n: (b * nc + n, h)
    per_head = lambda b, h, n: (h, 0, 0)
    return pl.pallas_call(
        _retention_kernel,
        grid=(batch, RET_HEADS, nc),
        in_specs=[
            pl.BlockSpec((c, RET_QK_DIM), row),
            pl.BlockSpec((c, RET_QK_DIM), lambda b, h, n: (b * nc + n, RET_HEADS + h)),
            pl.BlockSpec((c, RET_V_DIM), row),
            pl.BlockSpec((c, RET_V_DIM), row),
            pl.BlockSpec((c, RET_QK_DIM // 2), lambda b, h, n: (n, 0)),
            pl.BlockSpec((c, RET_QK_DIM // 2), lambda b, h, n: (n, 0)),
            pl.BlockSpec((None, c, c), per_head),
            pl.BlockSpec((None, c, V7X_LANES), per_head),
            pl.BlockSpec((None, c, V7X_LANES), per_head),
            pl.BlockSpec((None, 1, RET_V_DIM), per_head),
            pl.BlockSpec((1, RET_V_DIM), lambda b, h, n: (0, h)),
        ],
        out_specs=pl.BlockSpec((c, RET_V_DIM), row),
        out_shape=jax.ShapeDtypeStruct(v.shape, BF16),
        scratch_shapes=[pltpu.VMEM((RET_QK_DIM, RET_V_DIM), F32)],
        compiler_params=_params("parallel", "parallel", "arbitrary"),
        name="retention_core",
    )(qk, qk, v, g, cos, sin, dmask, xi, zeta, cdecay, gn.reshape(1, -1))


def _attention_bias(rel_bias):
    i = jnp.arange(ATT_QBLOCK)[:, None]
    m = jnp.arange(ATT_WINDOW)[None, :]
    period = ATT_QBLOCK + ATT_WINDOW - 1
    lag = jnp.arange(period)
    lag = jnp.where(lag < ATT_WINDOW, lag, lag - period)
    rel = jnp.clip(LEFT_CHUNKS * REF_CHUNK - lag, -REL_CLIP, REL_CLIP) + REL_CLIP
    first = rel_bias.astype(F32)[:, rel]
    heads = rel_bias.shape[0]
    skew = jnp.tile(first, (1, ATT_QBLOCK + 1))[:, :ATT_QBLOCK * (period - 1)]
    toeplitz = skew.reshape(heads, ATT_QBLOCK, period - 1)[:, :, :ATT_WINDOW]
    ci = i // REF_CHUNK
    cm = m // REF_CHUNK
    band = (cm >= ci) & (cm <= ci + LEFT_CHUNKS)
    return jnp.where(band[None], toeplitz, MASK_VALUE)


def _attention_kernel(q_ref, k0_ref, k1_ref, k2_ref, v0_ref, v1_ref, v2_ref, bias_ref, o_ref):
    qblk = pl.program_id(1)
    n_prev = LEFT_CHUNKS * REF_CHUNK // ATT_QBLOCK
    col = lax.broadcasted_iota(jnp.int32, (1, ATT_WINDOW), 1)
    start_mask = jnp.where(col >= (n_prev - qblk) * ATT_QBLOCK, 0.0, MASK_VALUE)
    k_refs = (k0_ref, k1_ref, k2_ref)
    v_refs = (v0_ref, v1_ref, v2_ref)

    def one_head(h, carry):
        cols = pl.ds(pl.multiple_of(h * ATT_HEAD_DIM, ATT_HEAD_DIM), ATT_HEAD_DIM)
        q = q_ref[:, cols]
        s = jnp.concatenate(
            [lax.dot_general(q, kr[:, cols], NT_DIMS, preferred_element_type=F32) for kr in k_refs], axis=-1)
        s = s + bias_ref[h] + start_mask
        m = jnp.max(s, axis=-1, keepdims=True)
        p = jnp.exp(s - m)
        p = (p / jnp.sum(p, axis=-1, keepdims=True)).astype(BF16)
        o = jnp.dot(p[:, :ATT_QBLOCK], v_refs[0][:, cols], preferred_element_type=F32)
        for w in range(1, len(v_refs)):
            o = o + jnp.dot(p[:, w * ATT_QBLOCK:(w + 1) * ATT_QBLOCK], v_refs[w][:, cols],
                            preferred_element_type=F32)
        o_ref[:, cols] = o.astype(o_ref.dtype)
        return carry

    lax.fori_loop(0, ATT_HEADS, one_head, 0, unroll=ATT_HEAD_UNROLL)


def attention_core(q, kv, bias, batch, seq):
    t, d = q.shape
    nq = seq // ATT_QBLOCK
    n_prev = LEFT_CHUNKS * REF_CHUNK // ATT_QBLOCK
    assert ATT_WINDOW == (n_prev + 1) * ATT_QBLOCK

    def window(w, part):
        return pl.BlockSpec((ATT_QBLOCK, d), lambda b, n: (b * nq + jnp.maximum(n - n_prev + w, 0), part))

    return pl.pallas_call(
        _attention_kernel,
        grid=(batch, nq),
        in_specs=[pl.BlockSpec((ATT_QBLOCK, d), lambda b, n: (b * nq + n, 0))]
        + [window(w, 0) for w in range(n_prev + 1)]
        + [window(w, 1) for w in range(n_prev + 1)]
        + [pl.BlockSpec(bias.shape, lambda b, n: (0, 0, 0))],
        out_specs=pl.BlockSpec((ATT_QBLOCK, d), lambda b, n: (b * nq + n, 0)),
        out_shape=jax.ShapeDtypeStruct((t, d), BF16),
        compiler_params=_params("parallel", "arbitrary"),
        name="attention_core",
    )(q, kv, kv, kv, kv, kv, kv, bias)


PEER_RANKS = PEER_TOPK + 1
V7X_SUBLANES = 8


def _sorting_network(n):
    pairs = []
    p = 1
    while p < n:
        k = p
        while k >= 1:
            for j in range(k % p, n - k, 2 * k):
                for i in range(min(k, n - j - k)):
                    if (i + j) // (2 * p) == (i + j + k) // (2 * p):
                        pairs.append((i + j, i + j + k))
            k //= 2
        p *= 2
    return pairs


def _sort_descending(vals):
    vals = list(vals)
    for i, j in _sorting_network(len(vals)):
        vals[i], vals[j] = jnp.maximum(vals[i], vals[j]), jnp.minimum(vals[i], vals[j])
    return vals


def _top_ranked(s):
    n = PEER_TOPK
    tiles = [s[r * V7X_SUBLANES:(r + 1) * V7X_SUBLANES, :] for r in range(s.shape[0] // V7X_SUBLANES)]
    assert len(tiles) == n
    top = _sort_descending(tiles)
    shift = V7X_SUBLANES // 2
    while shift >= 1:
        other = [pltpu.roll(v, shift, 0) for v in top]
        top = [jnp.maximum(top[i], other[n - 1 - i]) for i in range(n)]
        k = n // 2
        while k >= 1:
            for i in range(n):
                if i & k == 0:
                    top[i], top[i + k] = jnp.maximum(top[i], top[i + k]), jnp.minimum(top[i], top[i + k])
            k //= 2
        shift //= 2
    below = [jnp.where(t < top[n - 1], t, NEG_INF) for t in tiles]
    while len(below) > 1:
        below = [jnp.maximum(below[2 * i], below[2 * i + 1]) for i in range(len(below) // 2)]
    nxt = below[0]
    shift = V7X_SUBLANES // 2
    while shift >= 1:
        nxt = jnp.maximum(nxt, pltpu.roll(nxt, shift, 0))
        shift //= 2
    return top + [nxt]


def _bf16_twice(x):
    hi = pltpu.bitcast(x.astype(BF16).astype(F32), jnp.uint32)
    return hi | (hi >> 16)


def _peer_prep_kernel(x_ref, g_ref, wqt_ref, sk_ref, hnt_ref, b2_ref, e2_ref, n1_ref, c_ref, qt_ref):
    tm = x_ref.shape[0]
    lane_groups = tm // V7X_LANES
    assert lane_groups <= V7X_SUBLANES
    hnt = _rmsnorm_rows(x_ref[...], g_ref[...]).T.astype(BF16)
    hnt_ref[...] = hnt
    qt_ref[...] = jnp.dot(wqt_ref[...], hnt, preferred_element_type=F32)
    sublane = lax.broadcasted_iota(jnp.int32, (V7X_SUBLANES, V7X_LANES), 0)

    def one_head(h, carry):
        scores = []
        ranked = []
        for part in range(2):
            rows = pl.ds(pl.multiple_of(h * (2 * PEER_HALF) + part * PEER_HALF, PEER_HALF), PEER_HALF)
            s = jnp.dot(sk_ref[part], qt_ref[rows, :].astype(BF16), preferred_element_type=F32)
            scores.append(s)
            per_group = [_top_ranked(s[:, g * V7X_LANES:(g + 1) * V7X_LANES]) for g in range(lane_groups)]
            dense = []
            for r in range(PEER_RANKS):
                tile = per_group[lane_groups - 1][r]
                for g in range(lane_groups - 2, -1, -1):
                    tile = jnp.where(sublane == g, per_group[g][r], tile)
                dense.append(tile)
            ranked.append(dense)

        xs, ys = ranked
        sums = [xs[i] + ys[j] for i in range(PEER_RANKS) for j in range(PEER_RANKS) if (i + 1) * (j + 1) <= PEER_RANKS]
        size = 1
        while size < len(sums):
            size *= 2
        sums = _sort_descending(sums + [jnp.full_like(sums[0], NEG_INF)] * (size - len(sums)))
        z = jnp.ones_like(sums[0])
        for v in sums[1:PEER_TOPK]:
            z = z + jnp.exp(v - sums[0])
        tau = 0.5 * (sums[PEER_TOPK - 1] + sums[PEER_TOPK])
        half_over_z = 0.5 / z

        passing = []
        for i in range(PEER_TOPK):
            thr_i = tau - xs[i]
            count = jnp.where(ys[0] >= thr_i, 1.0, 0.0)
            for r in range(1, PEER_TOPK):
                count = count + jnp.where(ys[r] >= thr_i, 1.0, 0.0)
            passing.append(count)
        s1, s2 = scores
        for g in range(lane_groups):
            lanes = slice(g * V7X_LANES, (g + 1) * V7X_LANES)
            row = lambda t, g=g: t[g:g + 1, :]
            s1_g = s1[:, lanes]
            s2_g = s2[:, lanes]
            b2 = jnp.full_like(s2_g, float(PEER_TOPK))
            n1 = jnp.zeros_like(s1_g)
            for r in reversed(range(PEER_TOPK)):
                b2 = jnp.where(s2_g == row(ys[r]), float(r), b2)
                n1 = jnp.where(s1_g == row(xs[r]), row(passing[r]), n1)
            c = jnp.exp(s1_g - row(xs[0])) * row(half_over_z)
            n1_ref[h, :, lanes] = _bf16_twice(n1)
            c_ref[h, :, lanes] = _bf16_twice(c)
            b2_ref[h, g] = pltpu.bitcast(b2.astype(BF16), jnp.uint32)
            e2_ref[h, g] = pltpu.bitcast(jnp.exp(s2_g - row(ys[0])).astype(BF16), jnp.uint32)
        return carry

    lax.fori_loop(0, PEER_HEADS, one_head, 0)


def peer_prep(x, gain, wq_t, sub_keys, *, tm=512):
    t, d = x.shape
    tm = min(tm, t)
    assert t % tm == 0 and tm % V7X_LANES == 0
    nq = wq_t.shape[0]
    fac = jax.ShapeDtypeStruct((PEER_HEADS, PEER_NKEYS, t), jnp.uint32)
    fac_spec = pl.BlockSpec((PEER_HEADS, PEER_NKEYS, tm), lambda i: (0, 0, i))
    blocked = jax.ShapeDtypeStruct((PEER_HEADS, t // V7X_LANES, PEER_NKEYS // 2, V7X_LANES), jnp.uint32)
    blocked_spec = pl.BlockSpec((PEER_HEADS, tm // V7X_LANES, PEER_NKEYS // 2, V7X_LANES), lambda i: (0, i, 0, 0))
    return pl.pallas_call(
        _peer_prep_kernel,
        grid=(t // tm,),
        in_specs=[
            pl.BlockSpec((tm, d), lambda i: (i, 0)),
            pl.BlockSpec((1, d), lambda i: (0, 0)),
            pl.BlockSpec((nq, d), lambda i: (0, 0)),
            pl.BlockSpec(sub_keys.shape, lambda i: (0, 0, 0)),
        ],
        out_specs=[pl.BlockSpec((d, tm), lambda i: (0, i)), blocked_spec, blocked_spec, fac_spec, fac_spec],
        out_shape=[jax.ShapeDtypeStruct((d, t), BF16), blocked, blocked, fac, fac],
        scratch_shapes=[pltpu.VMEM((nq, tm), F32)],
        compiler_params=_params("parallel"),
        name="peer_prep",
    )(x, gain.reshape(1, d), wq_t, sub_keys)


PEER_KEYS_PER_STEP = 8
PEER_ETILE = PEER_KEYS_PER_STEP * PEER_NKEYS
PEER_SUBROWS = 16


def _peer_main_kernel(hnt_first_ref, u_first_ref, hnt_ref, u_ref, vt_ref, b2_ref, e2_ref, n1_ref, c_ref, x_ref,
                      *refs, tiles_per_block, normalize_output):
    gain_ref = refs[0] if normalize_output else None
    o_ref, acc_ref, at_ref, p_ref = refs[-4:]
    g = pl.program_id(0)
    j = lax.rem(g, tiles_per_block)
    tm = hnt_ref.shape[1]
    lane_groups = tm // V7X_LANES

    halves = 2
    keys_per_half = PEER_KEYS_PER_STEP // halves
    rows_per_half = PEER_ETILE // halves
    words = PEER_SUBROWS // 2
    word_tile = (words, V7X_LANES)

    def packed(x):
        return pltpu.bitcast(x, BF16)

    def pre_activation(u_tile_ref, hn_ref, slot):
        at_ref[slot] = jnp.dot(u_tile_ref[...], hn_ref[...], preferred_element_type=F32)

    @pl.when(g == 0)
    def _():
        pre_activation(u_first_ref, hnt_first_ref, 0)

    @pl.when(j == 0)
    def _():
        acc_ref[...] = jnp.zeros_like(acc_ref)

    def weighted_gelu(e, slot):
        for ka in range(e * keys_per_half, (e + 1) * keys_per_half):
            for lg in range(lane_groups):
                lanes = slice(lg * V7X_LANES, (lg + 1) * V7X_LANES)
                n1 = [packed(jnp.broadcast_to(n1_ref[h, ka:ka + 1, lanes], word_tile)) for h in range(PEER_HEADS)]
                cw = [packed(jnp.broadcast_to(c_ref[h, ka:ka + 1, lanes], word_tile)) for h in range(PEER_HEADS)]
                for bg in range(PEER_NKEYS // PEER_SUBROWS):
                    wrows = slice(bg * words, (bg + 1) * words)
                    w = None
                    for h in range(PEER_HEADS):
                        term = jnp.where(packed(b2_ref[h, lg, wrows, :]) < n1[h],
                                         packed(e2_ref[h, lg, wrows, :]) * cw[h], 0.0)
                        w = term if w is None else w + term
                    first = ka * PEER_NKEYS + bg * PEER_SUBROWS
                    pre = at_ref[slot, first:first + PEER_SUBROWS, lanes]
                    act = pre * (1.0 + lax.erf(pre * (2.0 ** -0.5)))
                    p_ref[lg, first // 2:first // 2 + words, :] = pltpu.bitcast(act.astype(BF16) * w, jnp.uint32)

    def value_matmul(e):
        rows = slice(e * rows_per_half, (e + 1) * rows_per_half)
        wrows = slice(e * rows_per_half // 2, (e + 1) * rows_per_half // 2)
        p_half = jnp.concatenate([packed(p_ref[lg, wrows, :]) for lg in range(lane_groups)], axis=-1)
        acc_ref[...] += jnp.dot(vt_ref[:, rows], p_half, preferred_element_type=F32)

    def step(slot):
        pre_activation(u_ref, hnt_ref, 1 - slot)
        weighted_gelu(0, slot)
        value_matmul(0)
        weighted_gelu(1, slot)
        value_matmul(1)

    for parity in range(2):
        pl.when(lax.rem(g, 2) == parity)(functools.partial(step, parity))

    @pl.when(j == tiles_per_block - 1)
    def _():
        y = x_ref[...] + acc_ref[...].T
        o_ref[...] = _rmsnorm_rows(y, gain_ref[...]) if normalize_output else y


def peer_main(hn_t, u, v_t, b2, e2, n1, c, x, *, out_gain=None, tm=512):
    t, d = x.shape
    n_exp = u.shape[0]
    assert v_t.shape == (n_exp // PEER_ETILE, d, PEER_ETILE)
    tm = min(tm, t)
    assert t % tm == 0 and tm % V7X_LANES == 0
    assert n_exp % PEER_ETILE == 0 and n_exp == PEER_NKEYS * PEER_NKEYS
    lane_groups = tm // V7X_LANES
    n_i = t // tm
    n_j = n_exp // PEER_ETILE
    assert n_j % 2 == 0

    def block(g):
        return g // n_j

    def tile(g):
        return lax.rem(g, n_j)

    def nxt(g):
        return jnp.minimum(g + 1, n_i * n_j - 1)

    all_keys = pl.BlockSpec((PEER_HEADS, lane_groups, PEER_NKEYS // 2, V7X_LANES), lambda g: (0, block(g), 0, 0))
    step_keys = pl.BlockSpec((PEER_HEADS, PEER_KEYS_PER_STEP, tm), lambda g: (0, tile(g), block(g)))
    gain_specs = [] if out_gain is None else [pl.BlockSpec((1, d), lambda g: (0, 0))]
    gain_args = [] if out_gain is None else [out_gain.reshape(1, d)]
    return pl.pallas_call(
        functools.partial(_peer_main_kernel, tiles_per_block=n_j, normalize_output=out_gain is not None),
        grid=(n_i * n_j,),
        in_specs=[
            pl.BlockSpec((d, tm), lambda g: (0, 0), pipeline_mode=pl.Buffered(1)),
            pl.BlockSpec((PEER_ETILE, d), lambda g: (0, 0), pipeline_mode=pl.Buffered(1)),
            pl.BlockSpec((d, tm), lambda g: (0, block(nxt(g)))),
            pl.BlockSpec((PEER_ETILE, d), lambda g: (tile(nxt(g)), 0)),
            pl.BlockSpec((None, d, PEER_ETILE), lambda g: (tile(g), 0, 0)),
            all_keys, all_keys, step_keys, step_keys,
            pl.BlockSpec((tm, d), lambda g: (block(g), 0)),
        ] + gain_specs,
        out_specs=pl.BlockSpec((tm, d), lambda g: (block(g), 0)),
        out_shape=jax.ShapeDtypeStruct((t, d), F32),
        scratch_shapes=[
            pltpu.VMEM((d, tm), F32),
            pltpu.VMEM((2, PEER_ETILE, tm), F32),
            pltpu.VMEM((lane_groups, PEER_ETILE // 2, V7X_LANES), jnp.uint32),
        ],
        compiler_params=_params("arbitrary"),
        name="peer_main",
    )(hn_t, u, hn_t, u, v_t, b2, e2, n1, c, x, *gain_args)


def peer_layer(x, gain, wq_t, sub_keys, u, v_t, out_gain=None):
    hn_t, b2, e2, n1, c = peer_prep(x, gain, wq_t, sub_keys)
    return peer_main(hn_t, u, v_t, b2, e2, n1, c, x, out_gain=out_gain)


def kernel(x, ln_mix, ln_ffn, ret_w_in, ret_w_out, ret_gn, kv_norm, w_kv, att_w_q, att_w_o, att_rel_bias,
           peer_w_q, peer_sub_keys, peer_u, peer_v, ln_final):
    batch, seq, d = x.shape
    depth = ln_mix.shape[0]
    n_a = ret_w_in.shape[0]
    xt = x.reshape(batch * seq, d)
    qk_width = 2 * RET_HEADS * RET_QK_DIM
    v_width = RET_HEADS * RET_V_DIM
    kv = None
    w_in_all = ret_w_in.astype(BF16)
    for l in range(depth):
        if l < n_a:
            qk, v, g = norm_matmul_split(xt, ln_mix[l], w_in_all, l, (qk_width, v_width, v_width), (F32, BF16, F32))
            y = retention_core(qk, v, g, ret_gn[l], batch, seq)
            xt = matmul_residual(y, ret_w_out[l].astype(BF16), xt)
        else:
            j = l - n_a
            q = norm_matmul(xt, ln_mix[l], att_w_q[j].astype(BF16), col_start=0, n_cols=d, out_dtype=BF16,
                            scale=ATT_HEAD_DIM ** -0.5)
            a = attention_core(q, kv, _attention_bias(att_rel_bias[j]), batch, seq)
            xt = matmul_residual(a, att_w_o[j].astype(BF16), xt)
        xt = peer_layer(xt, ln_ffn[l], peer_w_q[l].T.astype(BF16), peer_sub_keys[l].astype(BF16),
                        peer_u[l].astype(BF16),
                        peer_v[l].reshape(-1, PEER_ETILE, d).transpose(0, 2, 1).astype(BF16),
                        out_gain=ln_final if l == depth - 1 else None)
        if l == n_a - 1:
            assert l < depth - 1
            kv = norm_matmul(xt, kv_norm, w_kv.astype(BF16), col_start=0, n_cols=2 * d, out_dtype=BF16)
    return xt.reshape(batch, seq, d)
```
